```python
import math
import jax
import jax.numpy as jnp
from jax import lax
import numpy as np

D_MODEL = 2048
BATCH = 1
SEQ = 16384
DEPTH = 2

MEM_LEN = 256
EPS = 1e-6
MLA_V = 128
MLA_HEADS = (D_MODEL // 2) // MLA_V
MLA_NOPE = 128
MLA_ROPE = 64
MLA_QK = MLA_NOPE + MLA_ROPE
Q_LORA = 512
KV_LORA = 512
ROPE_BASE = 10000.0
S5_WIDTH = D_MODEL // 2
S5_GROUP = 16
S5_GROUPS = S5_WIDTH // S5_GROUP
S5_STATE = 64
DT_MIN = 1e-3
DT_MAX = 1e-1
MOBA_HEADS = 16
MOBA_HD = D_MODEL // MOBA_HEADS
MOBA_BLOCK = 256
MOBA_TOPK = 3
MOBA_QCHUNK = 32
ATTN_QBLOCK = 128
XA_HEADS = 4
XA_HD = 128
D_FF = 4 * D_MODEL

kernel_name = 'hybrid_mla_s5_moba_trunk'


def rms_norm(x, g):
    xf = x.astype(jnp.float32)
    y = xf * lax.rsqrt(jnp.mean(xf * xf, axis=-1, keepdims=True) + EPS)
    return (y * g.astype(jnp.float32)).astype(x.dtype)


def rope_tables(seq_len):
    inv_freq = 1.0 / (ROPE_BASE ** (jnp.arange(0, MLA_ROPE, 2, dtype=jnp.float32) / MLA_ROPE))
    ang = jnp.arange(seq_len, dtype=jnp.float32)[:, None] * inv_freq[None, :]
    return jnp.cos(ang), jnp.sin(ang)


def apply_rope(x, cos, sin):
    xf = x.astype(jnp.float32)
    x1, x2 = jnp.split(xf, 2, axis=-1)
    return jnp.concatenate([x1 * cos - x2 * sin, x1 * sin + x2 * cos], axis=-1).astype(x.dtype)


def alibi_slopes(n_heads):
    return 2.0 ** (-8.0 * jnp.arange(1, n_heads + 1, dtype=jnp.float32) / n_heads)


def causal_softmax_attention(q, k, v, scale):
    b, h, s, _ = q.shape
    nq = s // ATTN_QBLOCK
    q_blocks = q.reshape(b, h, nq, ATTN_QBLOCK, -1).transpose(2, 0, 1, 3, 4)
    k_pos = jnp.arange(s)

    def one_block(args):
        i, qi = args
        q_pos = i * ATTN_QBLOCK + jnp.arange(ATTN_QBLOCK)
        sc = jnp.einsum('bhqd,bhkd->bhqk', qi, k, preferred_element_type=jnp.float32) * scale
        sc = jnp.where(k_pos[None, :] <= q_pos[:, None], sc, -jnp.inf)
        p = jax.nn.softmax(sc, axis=-1).astype(v.dtype)
        return jnp.einsum('bhqk,bhkd->bhqd', p, v)

    out = lax.map(one_block, (jnp.arange(nq), q_blocks))
    return out.transpose(1, 2, 0, 3, 4).reshape(b, h, s, -1)


def mla_heads(c_q, c_kv, k_rope, cos, sin, q_lora_norm, w_uq, kv_lora_norm, w_ukv, q_norm, k_norm):
    b, s, _ = c_q.shape
    q = (rms_norm(c_q, q_lora_norm) @ w_uq).reshape(b, s, MLA_HEADS, MLA_QK)
    q = jnp.concatenate([q[..., :MLA_NOPE],
                         apply_rope(q[..., MLA_NOPE:], cos[None, :, None, :], sin[None, :, None, :])], axis=-1)
    kv = (rms_norm(c_kv, kv_lora_norm) @ w_ukv).reshape(b, s, MLA_HEADS, MLA_NOPE + MLA_V)
    k_r = apply_rope(k_rope, cos[None], sin[None])
    k = jnp.concatenate([kv[..., :MLA_NOPE],
                         jnp.broadcast_to(k_r[:, :, None, :], (b, s, MLA_HEADS, MLA_ROPE))], axis=-1)
    v = kv[..., MLA_NOPE:]
    q = rms_norm(q, q_norm)
    k = rms_norm(k, k_norm)
    o = causal_softmax_attention(q.transpose(0, 2, 1, 3), k.transpose(0, 2, 1, 3),
                                 v.transpose(0, 2, 1, 3), MLA_QK ** -0.5)
    return o.transpose(0, 2, 1, 3).reshape(b, s, MLA_HEADS * MLA_V)


def s5_groups(u, a_re, a_im, log_dt, b_re, b_im, c_re, c_im, d_skip, glu_val, glu_gate):
    b, s, _ = u.shape
    uf = u.astype(jnp.float32).reshape(b, s, S5_GROUPS, S5_GROUP)
    lam = lax.complex(a_re.astype(jnp.float32), a_im.astype(jnp.float32))
    dt = jnp.exp(log_dt.astype(jnp.float32))[:, None]
    lam_bar = jnp.exp(lam * dt)
    b_bar = ((lam_bar - 1.0) / lam)[:, :, None] * lax.complex(b_re.astype(jnp.float32),
                                                              b_im.astype(jnp.float32))
    bu = lax.complex(jnp.einsum('gnp,bsgp->bsgn', b_bar.real, uf),
                     jnp.einsum('gnp,bsgp->bsgn', b_bar.imag, uf))

    def combine(left, right):
        a_l, x_l = left
        a_r, x_r = right
        return a_l * a_r, a_r * x_l + x_r

    _, state = lax.associative_scan(combine, (jnp.broadcast_to(lam_bar, bu.shape), bu), axis=1)
    y = (jnp.einsum('gpn,bsgn->bsgp', c_re.astype(jnp.float32), state.real)
         - jnp.einsum('gpn,bsgn->bsgp', c_im.astype(jnp.float32), state.imag)
         + d_skip.astype(jnp.float32).reshape(S5_GROUPS, S5_GROUP) * uf)
    y = jax.nn.gelu(y)
    val = jnp.einsum('gpq,bsgp->bsgq', glu_val.astype(jnp.float32), y)
    gate = jnp.einsum('gpq,bsgp->bsgq', glu_gate.astype(jnp.float32), y)
    return (val * jax.nn.sigmoid(gate)).reshape(b, s, S5_WIDTH).astype(u.dtype)


def mla_s5_mixer(h, cos, sin, w_in, q_lora_norm, w_uq, kv_lora_norm, w_ukv, mla_q_norm, mla_k_norm,
                 a_re, a_im, log_dt, b_re, b_im, c_re, c_im, d_skip, glu_val, glu_gate, w_out):
    proj = h @ w_in
    c_q, c_kv, k_rope, u = jnp.split(proj, [Q_LORA, Q_LORA + KV_LORA, Q_LORA + KV_LORA + MLA_ROPE], axis=-1)
    y_a = mla_heads(c_q, c_kv, k_rope, cos, sin, q_lora_norm, w_uq, kv_lora_norm, w_ukv, mla_q_norm, mla_k_norm)
    y_b = s5_groups(u, a_re, a_im, log_dt, b_re, b_im, c_re, c_im, d_skip, glu_val, glu_gate)
    return jnp.concatenate([y_a, y_b], axis=-1) @ w_out


def moba_attention(q, k, v, slopes):
    b, h, s, hd = q.shape
    n_blk = -(-s // MOBA_BLOCK)
    s_pad = n_blk * MOBA_BLOCK
    if s_pad != s:
        pad = ((0, 0), (0, 0), (0, s_pad - s), (0, 0))
        q, k, v = jnp.pad(q, pad), jnp.pad(k, pad), jnp.pad(v, pad)
    k_blk = k.reshape(b, h, n_blk, MOBA_BLOCK, hd)
    v_blk = v.reshape(b, h, n_blk, MOBA_BLOCK, hd)
    q_blk_id = jnp.arange(s_pad) // MOBA_BLOCK
    gate = jnp.einsum('bhsd,bhnd->bhsn', q.astype(jnp.float32),
                      jnp.mean(k_blk.astype(jnp.float32), axis=3))
    fully_past = jnp.arange(n_blk)[None, :] < q_blk_id[:, None]
    gate = jnp.where(fully_past, gate, -jnp.inf)
    top = min(MOBA_TOPK, n_blk)
    _, sel = lax.top_k(gate, top)
    n_chunk = s_pad // MOBA_QCHUNK
    q_chunks = q.reshape(b, h, n_chunk, MOBA_QCHUNK, hd).transpose(2, 0, 1, 3, 4)
    sel_chunks = sel.reshape(b, h, n_chunk, MOBA_QCHUNK, top).transpose(2, 0, 1, 3, 4)
    b_idx = jnp.arange(b)[:, None, None, None]
    h_idx = jnp.arange(h)[None, :, None, None]
    offs = jnp.arange(MOBA_BLOCK)
    scale = hd ** -0.5

    def one_chunk(args):
        c, qi, si = args
        q_pos = c * MOBA_QCHUNK + jnp.arange(MOBA_QCHUNK)
        own = (c * MOBA_QCHUNK) // MOBA_BLOCK
        k_own = lax.dynamic_index_in_dim(k_blk, own, axis=2, keepdims=False)
        v_own = lax.dynamic_index_in_dim(v_blk, own, axis=2, keepdims=False)
        dist_own = (q_pos[:, None] - (own * MOBA_BLOCK + offs)[None, :]).astype(jnp.float32)
        s_own = (jnp.einsum('bhqd,bhkd->bhqk', qi, k_own, preferred_element_type=jnp.float32) * scale
                 - slopes[None, :, None, None] * dist_own)
        s_own = jnp.where(dist_own >= 0, s_own, -jnp.inf)
        k_sel = k_blk[b_idx, h_idx, si]
        v_sel = v_blk[b_idx, h_idx, si]
        kpos_sel = si[..., None] * MOBA_BLOCK + offs
        dist_sel = (q_pos[:, None, None] - kpos_sel).astype(jnp.float32)
        s_sel = (jnp.einsum('bhqd,bhqjkd->bhqjk', qi, k_sel, preferred_element_type=jnp.float32) * scale
                 - slopes[None, :, None, None, None] * dist_sel)
        valid = jnp.arange(top)[:, None] < (q_pos // MOBA_BLOCK)[:, None, None]
        s_sel = jnp.where(valid, s_sel, -jnp.inf)
        sc = jnp.concatenate([s_sel.reshape(b, h, MOBA_QCHUNK, top * MOBA_BLOCK), s_own], axis=-1)
        p = jax.nn.softmax(sc, axis=-1).astype(v.dtype)
        p_sel = p[..., :top * MOBA_BLOCK].reshape(b, h, MOBA_QCHUNK, top, MOBA_BLOCK)
        p_own = p[..., top * MOBA_BLOCK:]
        return (jnp.einsum('bhqjk,bhqjkd->bhqd', p_sel, v_sel)
                + jnp.einsum('bhqk,bhkd->bhqd', p_own, v_own))

    out = lax.map(one_chunk, (jnp.arange(n_chunk), q_chunks, sel_chunks))
    return out.transpose(1, 2, 0, 3, 4).reshape(b, h, s_pad, hd)[:, :, :s]


def moba_mixer(h, w_qkv, q_norm, k_norm, w_o):
    b, s, _ = h.shape
    qkv = (h @ w_qkv).reshape(b, s, 3, MOBA_HEADS, MOBA_HD)
    q = rms_norm(qkv[:, :, 0], q_norm).transpose(0, 2, 1, 3)
    k = rms_norm(qkv[:, :, 1], k_norm).transpose(0, 2, 1, 3)
    v = qkv[:, :, 2].transpose(0, 2, 1, 3)
    o = moba_attention(q, k, v, alibi_slopes(MOBA_HEADS))
    return o.transpose(0, 2, 1, 3).reshape(b, s, MOBA_HEADS * MOBA_HD) @ w_o


def memory_cross_attention(x, mem, norm_g, mem_norm_g, w_q, w_kv, q_norm, k_norm, w_o):
    b, s, _ = x.shape
    m_len = mem.shape[1]
    q = (rms_norm(x, norm_g) @ w_q).reshape(b, s, XA_HEADS, XA_HD)
    kv = (rms_norm(mem, mem_norm_g) @ w_kv).reshape(b, m_len, 2, XA_HEADS, XA_HD)
    q = rms_norm(q, q_norm)
    k = rms_norm(kv[:, :, 0], k_norm)
    v = kv[:, :, 1]
    sc = jnp.einsum('bshd,bmhd->bhsm', q, k, preferred_element_type=jnp.float32) * XA_HD ** -0.5
    p = jax.nn.softmax(sc, axis=-1).astype(v.dtype)
    o = jnp.einsum('bhsm,bmhd->bshd', p, v).reshape(b, s, XA_HEADS * XA_HD)
    return o @ w_o


def sq_relu_mlp(x, norm_g, w_up, w_down):
    hid = rms_norm(x, norm_g) @ w_up
    return jnp.square(jax.nn.relu(hid)) @ w_down


def setup_inputs(seed: int = 0) -> dict:
    key = jax.random.key(seed)
    keys = iter(jax.random.split(key, 64))

    def w(shape, fan_in):
        return jax.random.normal(next(keys), shape, jnp.float32) * fan_in ** -0.5

    def g(n):
        return 1.0 + 0.01 * jax.random.normal(next(keys), (n,), jnp.float32)

    def noise(shape, scale):
        return scale * jax.random.normal(next(keys), shape, jnp.float32)

    G, N, P = S5_GROUPS, S5_STATE, S5_GROUP
    xa_w = XA_HEADS * XA_HD
    in_cols = Q_LORA + KV_LORA + MLA_ROPE + S5_WIDTH
    mix0 = MLA_HEADS * MLA_V + S5_WIDTH
    n_idx = jnp.arange(N, dtype=jnp.float32)
    return {
        'x': jax.random.normal(next(keys), (BATCH, SEQ, D_MODEL), jnp.float32),
        'mem': jax.random.normal(next(keys), (BATCH, MEM_LEN, D_MODEL), jnp.float32),
        'l0_mix_norm': g(D_MODEL),
        'l0_w_in': w((D_MODEL, in_cols), D_MODEL),
        'l0_q_lora_norm': g(Q_LORA),
        'l0_w_uq': w((Q_LORA, MLA_HEADS * MLA_QK), Q_LORA),
        'l0_kv_lora_norm': g(KV_LORA),
        'l0_w_ukv': w((KV_LORA, MLA_HEADS * (MLA_NOPE + MLA_V)), KV_LORA),
        'l0_mla_q_norm': g(MLA_QK),
        'l0_mla_k_norm': g(MLA_QK),
        'l0_s5_a_re': -0.5 + noise((G, N), 0.01),
        'l0_s5_a_im': math.pi * jnp.broadcast_to(n_idx, (G, N)) + noise((G, N), 0.01),
        'l0_s5_log_dt': jax.random.uniform(next(keys), (G,), jnp.float32, math.log(DT_MIN), math.log(DT_MAX)),
        'l0_s5_b_re': w((G, N, P), 2 * P),
        'l0_s5_b_im': w((G, N, P), 2 * P),
        'l0_s5_c_re': w((G, P, N), 2 * N),
        'l0_s5_c_im': w((G, P, N), 2 * N),
        'l0_s5_d': jax.random.normal(next(keys), (S5_WIDTH,), jnp.float32),
        'l0_s5_glu_val': w((G, P, P), P),
        'l0_s5_glu_gate': w((G, P, P), P),
        'l0_w_out': w((mix0, D_MODEL), mix0),
        'l0_xa_norm': g(D_MODEL),
        'l0_xa_mem_norm': g(D_MODEL),
        'l0_xa_w_q': w((D_MODEL, xa_w), D_MODEL),
        'l0_xa_w_kv': w((D_MODEL, 2 * xa_w), D_MODEL),
        'l0_xa_q_norm': g(XA_HD),
        'l0_xa_k_norm': g(XA_HD),
        'l0_xa_w_o': w((xa_w, D_MODEL), xa_w),
        'l0_mlp_norm': g(D_MODEL),
        'l0_w_up': w((D_MODEL, D_FF), D_MODEL),
        'l0_w_down': w((D_FF, D_MODEL), D_FF),
        'l1_mix_norm': g(D_MODEL),
        'l1_w_qkv': w((D_MODEL, 3 * MOBA_HEADS * MOBA_HD), D_MODEL),
        'l1_q_norm': g(MOBA_HD),
        'l1_k_norm': g(MOBA_HD),
        'l1_w_o': w((MOBA_HEADS * MOBA_HD, D_MODEL), MOBA_HEADS * MOBA_HD),
        'l1_xa_norm': g(D_MODEL),
        'l1_xa_mem_norm': g(D_MODEL),
        'l1_xa_w_q': w((D_MODEL, xa_w), D_MODEL),
        'l1_xa_w_kv': w((D_MODEL, 2 * xa_w), D_MODEL),
        'l1_xa_q_norm': g(XA_HD),
        'l1_xa_k_norm': g(XA_HD),
        'l1_xa_w_o': w((xa_w, D_MODEL), xa_w),
        'l1_mlp_norm': g(D_MODEL),
        'l1_w_up': w((D_MODEL, D_FF), D_MODEL),
        'l1_w_down': w((D_FF, D_MODEL), D_FF),
    }


def reference(x, mem,
              l0_mix_norm, l0_w_in, l0_q_lora_norm, l0_w_uq, l0_kv_lora_norm, l0_w_ukv,
              l0_mla_q_norm, l0_mla_k_norm, l0_s5_a_re, l0_s5_a_im, l0_s5_log_dt,
              l0_s5_b_re, l0_s5_b_im, l0_s5_c_re, l0_s5_c_im, l0_s5_d, l0_s5_glu_val, l0_s5_glu_gate,
              l0_w_out, l0_xa_norm, l0_xa_mem_norm, l0_xa_w_q, l0_xa_w_kv, l0_xa_q_norm, l0_xa_k_norm,
              l0_xa_w_o, l0_mlp_norm, l0_w_up, l0_w_down,
              l1_mix_norm, l1_w_qkv, l1_q_norm, l1_k_norm, l1_w_o,
              l1_xa_norm, l1_xa_mem_norm, l1_xa_w_q, l1_xa_w_kv, l1_xa_q_norm, l1_xa_k_norm, l1_xa_w_o,
              l1_mlp_norm, l1_w_up, l1_w_down):
    cos, sin = rope_tables(x.shape[1])
    for layer in range(DEPTH):
        if layer % 2 == 0:
            x = x + mla_s5_mixer(rms_norm(x, l0_mix_norm), cos, sin, l0_w_in, l0_q_lora_norm, l0_w_uq,
                                 l0_kv_lora_norm, l0_w_ukv, l0_mla_q_norm, l0_mla_k_norm,
                                 l0_s5_a_re, l0_s5_a_im, l0_s5_log_dt, l0_s5_b_re, l0_s5_b_im,
                                 l0_s5_c_re, l0_s5_c_im, l0_s5_d, l0_s5_glu_val, l0_s5_glu_gate, l0_w_out)
            x = x + memory_cross_attention(x, mem, l0_xa_norm, l0_xa_mem_norm, l0_xa_w_q, l0_xa_w_kv,
                                           l0_xa_q_norm, l0_xa_k_norm, l0_xa_w_o)
            x = x + sq_relu_mlp(x, l0_mlp_norm, l0_w_up, l0_w_down)
        else:
            x = x + moba_mixer(rms_norm(x, l1_mix_norm), l1_w_qkv, l1_q_norm, l1_k_norm, l1_w_o)
            x = x + memory_cross_attention(x, mem, l1_xa_norm, l1_xa_mem_norm, l1_xa_w_q, l1_xa_w_kv,
                                           l1_xa_q_norm, l1_xa_k_norm, l1_xa_w_o)
            x = x + sq_relu_mlp(x, l1_mlp_norm, l1_w_up, l1_w_down)
    return x
```

```python
import functools
import math

import jax
import jax.numpy as jnp
from jax import lax
from jax.experimental import pallas as pl
from jax.experimental.pallas import tpu as pltpu

F32 = jnp.float32
BF16 = jnp.bfloat16

D_MODEL = 2048
EPS = 1e-6
MASK_VALUE = -1e30

MLA_HEADS = 8
MLA_NOPE = 128
MLA_ROPE = 64
MLA_QK = MLA_NOPE + MLA_ROPE
MLA_V = 128
Q_LORA = 512
KV_LORA = 512
ROPE_BASE = 10000.0
S5_WIDTH = 1024
S5_GROUP = 16
S5_GROUPS = 64
S5_STATE = 64
S5_CHUNK = 16
S5_ROW = S5_CHUNK * S5_GROUP
MOBA_HEADS = 16
MOBA_HD = 128
MOBA_BLOCK = 256
MOBA_TOPK = 3
XA_HEADS = 4
XA_HD = 128
D_FF = 4 * D_MODEL

LANES = 128
HEAD_PAD = 256
VT_ROWS = 136
VMEM_LIMIT_BYTES = 56 * 1024 * 1024


def _cparams(*sem):
    return pltpu.CompilerParams(dimension_semantics=sem, vmem_limit_bytes=VMEM_LIMIT_BYTES)


def _rms(x, g):
    return x * lax.rsqrt(jnp.mean(x * x, axis=-1, keepdims=True) + EPS) * g


def _dot(a, b):
    return jnp.dot(a, b, preferred_element_type=F32)


def _dot_nt(a, b):
    return lax.dot_general(a, b, (((1,), (1,)), ((), ())), preferred_element_type=F32)


def _l0_in_kernel(x_ref, g_ref, w_ref, cq_ref, ckv_ref, kra_ref, krb_ref, u_ref):
    xn = _rms(x_ref[...], g_ref[...]).astype(BF16)
    p = _dot(xn, w_ref[...])
    cq_ref[...] = p[:, 0:512]
    ckv_ref[...] = p[:, 512:1024]
    kra_ref[...] = p[:, 1024:1152]
    krb_ref[...] = p[:, 1152:1280]
    u_ref[...] = p[:, 1280:2304]


def _l0_in(x, g, w):
    s = x.shape[0]
    tm = min(512, s)
    n = w.shape[1]
    row = lambda width: pl.BlockSpec((tm, width), lambda i: (i, 0))
    return pl.pallas_call(
        _l0_in_kernel,
        grid=(s // tm,),
        in_specs=[row(D_MODEL), pl.BlockSpec((1, D_MODEL), lambda i: (0, 0)),
                  pl.BlockSpec((D_MODEL, n), lambda i: (0, 0))],
        out_specs=[row(512), row(512), row(128), row(128), row(1024)],
        out_shape=[jax.ShapeDtypeStruct((s, 512), F32), jax.ShapeDtypeStruct((s, 512), F32),
                   jax.ShapeDtypeStruct((s, 128), F32), jax.ShapeDtypeStruct((s, 128), F32),
                   jax.ShapeDtypeStruct((s, 1024), F32)],
        compiler_params=_cparams("parallel"),
        name="l0_in_proj",
    )(x, g, w)


def _mla_qkv_kernel(cq_ref, ckv_ref, kra_ref, krb_ref, cos_ref, sin_ref, gq_ref, gkv_ref,
                    wuq_ref, wukv_ref, qn_ref, kn_ref, q_ref, k_ref, v_ref):
    cos = cos_ref[...]
    sin = sin_ref[...]
    qn = qn_ref[...]
    kn = kn_ref[...]
    qp = _dot(_rms(cq_ref[...], gq_ref[...]).astype(BF16), wuq_ref[...])
    kvp = _dot(_rms(ckv_ref[...], gkv_ref[...]).astype(BF16), wukv_ref[...])
    k_rot = kra_ref[...] * cos + krb_ref[...] * sin
    k_rot_ss = jnp.sum(k_rot * k_rot, axis=-1, keepdims=True)
    q_scale = MLA_QK ** -0.5
    for h in range(MLA_HEADS):
        qb = h * 384
        nope = qp[:, qb:qb + 128]
        rot = qp[:, qb + 128:qb + 256] * cos + qp[:, qb + 256:qb + 384] * sin
        ss = jnp.sum(nope * nope, axis=-1, keepdims=True) + jnp.sum(rot * rot, axis=-1, keepdims=True)
        inv = lax.rsqrt(ss * (1.0 / MLA_QK) + EPS) * q_scale
        q_ref[h, :, 0:128] = (nope * inv * qn[0:1, :]).astype(BF16)
        q_ref[h, :, 128:256] = (rot * inv * qn[1:2, :]).astype(BF16)
        kb = h * 256
        k_nope = kvp[:, kb:kb + 128]
        ssk = jnp.sum(k_nope * k_nope, axis=-1, keepdims=True) + k_rot_ss
        invk = lax.rsqrt(ssk * (1.0 / MLA_QK) + EPS)
        k_ref[h, :, 0:128] = (k_nope * invk * kn[0:1, :]).astype(BF16)
        k_ref[h, :, 128:256] = (k_rot * invk * kn[1:2, :]).astype(BF16)
        v_ref[h] = kvp[:, kb + 128:kb + 256].astype(BF16)


def _mla_qkv(cq, ckv, kra, krb, cos_t, sin_t, gq, gkv, wuq, wukv, qn, kn):
    s = cq.shape[0]
    tm = min(256, s)
    row = lambda width: pl.BlockSpec((tm, width), lambda i: (i, 0))
    full = lambda a: pl.BlockSpec(a.shape, lambda i: (0,) * a.ndim)
    head_out = lambda width: pl.BlockSpec((MLA_HEADS, tm, width), lambda i: (0, i, 0))
    return pl.pallas_call(
        _mla_qkv_kernel,
        grid=(s // tm,),
        in_specs=[row(512), row(512), row(128), row(128), row(128), row(128),
                  full(gq), full(gkv), full(wuq), full(wukv), full(qn), full(kn)],
        out_specs=[head_out(HEAD_PAD), head_out(HEAD_PAD), head_out(MLA_V)],
        out_shape=[jax.ShapeDtypeStruct((MLA_HEADS, s, HEAD_PAD), BF16),
                   jax.ShapeDtypeStruct((MLA_HEADS, s, HEAD_PAD), BF16),
                   jax.ShapeDtypeStruct((MLA_HEADS, s, MLA_V), BF16)],
        compiler_params=_cparams("parallel"),
        name="mla_qkv",
    )(cq, ckv, kra, krb, cos_t, sin_t, gq, gkv, wuq, wukv, qn, kn)


def _flash_kernel(slope_ref, q_ref, k_ref, vt_ref, o_ref, m_ref, acc_ref, *, tile, alibi):
    h = pl.program_id(0)
    qi = pl.program_id(1)
    q = q_ref[...]
    m_ref[...] = jnp.full(m_ref.shape, -jnp.inf, F32)
    acc_ref[...] = jnp.zeros(acc_ref.shape, F32)
    key_idx = lax.broadcasted_iota(jnp.int32, (tile, tile), 0)
    qry_idx = lax.broadcasted_iota(jnp.int32, (tile, tile), 1)
    if alibi:
        slope = slope_ref[h]
        rel_bias = slope * (key_idx - qry_idx).astype(F32)

    def step(kj, diagonal):
        st = _dot_nt(k_ref[kj], q)
        if alibi:
            st = st + (rel_bias + slope * ((kj - qi) * tile).astype(F32))
        if diagonal:
            st = jnp.where(key_idx <= qry_idx, st, MASK_VALUE)
        m_prev = m_ref[...]
        m_new = jnp.maximum(m_prev, jnp.max(st, axis=0, keepdims=True))
        alpha = jnp.exp(m_prev - m_new)
        p = jnp.exp(st - m_new).astype(BF16)
        acc_ref[...] = alpha * acc_ref[...] + _dot(vt_ref[kj], p)
        m_ref[...] = m_new

    def body(kj, carry):
        step(kj, False)
        return carry

    lax.fori_loop(0, qi, body, 0)
    step(qi, True)
    acc = acc_ref[...]
    out_t = acc[0:128, :] / acc[128:129, :]
    o_ref[...] = out_t.T.astype(o_ref.dtype)


def _flash(q, k, vt, slopes, *, tile, alibi):
    heads, s, _ = q.shape
    nk = s // tile
    kernel = functools.partial(_flash_kernel, tile=tile, alibi=alibi)
    return pl.pallas_call(
        kernel,
        grid_spec=pltpu.PrefetchScalarGridSpec(
            num_scalar_prefetch=1,
            grid=(heads, s // tile),
            in_specs=[pl.BlockSpec((None, tile, HEAD_PAD), lambda h, i, sl: (h, i, 0)),
                      pl.BlockSpec((None, nk, tile, HEAD_PAD), lambda h, i, sl: (h, 0, 0, 0)),
                      pl.BlockSpec((None, nk, VT_ROWS, tile), lambda h, i, sl: (h, 0, 0, 0))],
            out_specs=pl.BlockSpec((tile, 128), lambda h, i, sl: (i, h)),
            scratch_shapes=[pltpu.VMEM((1, tile), F32), pltpu.VMEM((VT_ROWS, tile), F32)],
        ),
        out_shape=jax.ShapeDtypeStruct((s, heads * 128), BF16),
        compiler_params=_cparams("parallel", "arbitrary"),
        name="flash_alibi" if alibi else "flash_causal",
    )(slopes, q, k, vt)


def _blocked_kv(k, v, tile):
    heads, s, _ = k.shape
    nk = s // tile
    kb = k.reshape(heads, nk, tile, HEAD_PAD)
    vt = jnp.swapaxes(v.reshape(heads, nk, tile, 128), 2, 3)
    ones = jnp.ones((heads, nk, 1, tile), v.dtype)
    zeros = jnp.zeros((heads, nk, VT_ROWS - 129, tile), v.dtype)
    return kb, jnp.concatenate([vt, ones, zeros], axis=2)


def _s5_kernel(u_ref, m_ref, wx_ref, wxs_ref, vo_ref, a1_ref, a2_ref, d_ref, gv_ref, gg_ref,
               o_ref, x_ref, xs_ref, st_ref):
    u = u_ref[...]
    ub = u.astype(BF16)
    x_ref[...] = _dot(ub, wx_ref[...])
    xs_ref[...] = _dot(ub, wxs_ref[...])
    a1 = a1_ref[...]
    a2 = a2_ref[...]
    n_chunks = u.shape[0]

    def body(c, carry):
        s, ss = carry
        st_ref[pl.ds(c, 1), :] = s
        x = x_ref[pl.ds(c, 1), :]
        xs = xs_ref[pl.ds(c, 1), :]
        return a1 * s + a2 * ss + x, a1 * ss - a2 * s + xs

    zero = jnp.zeros((1, 2 * S5_STATE), F32)
    lax.fori_loop(0, n_chunks, body, (zero, zero))
    y = _dot(ub, m_ref[...]) + _dot(st_ref[...].astype(BF16), vo_ref[...]) + d_ref[...] * u
    y = jax.nn.gelu(y).astype(BF16)
    val = _dot(y, gv_ref[...])
    gate = _dot(y, gg_ref[...])
    o_ref[...] = (val * jax.nn.sigmoid(gate)).astype(o_ref.dtype)


def _s5(u_rows, m, wx, wxs, vo, a1, a2, d, gv, gg):
    groups, n_chunks, _ = u_rows.shape
    per_group = lambda a: pl.BlockSpec((None,) + a.shape[1:], lambda g: (g,) + (0,) * (a.ndim - 1))
    args = (u_rows, m, wx, wxs, vo, a1, a2, d, gv, gg)
    return pl.pallas_call(
        _s5_kernel,
        grid=(groups,),
        in_specs=[per_group(a) for a in args],
        out_specs=pl.BlockSpec((None, n_chunks, S5_ROW), lambda g: (g, 0, 0)),
        out_shape=jax.ShapeDtypeStruct((groups, n_chunks, S5_ROW), BF16),
        scratch_shapes=[pltpu.VMEM((n_chunks, 2 * S5_STATE), F32)] * 3,
        compiler_params=_cparams("parallel"),
        name="s5_scan",
    )(*args)


def _s5_operators(a_re, a_im, log_dt, b_re, b_im, c_re, c_im, d_skip, glu_val, glu_gate):
    hp = lax.Precision.HIGHEST
    a_re, a_im = a_re.astype(F32), a_im.astype(F32)
    dt = jnp.exp(log_dt.astype(F32))[:, None]
    er, ei = a_re * dt, a_im * dt

    def lam_pow(k):
        kk = k.astype(F32)[:, None, None]
        mag = jnp.exp(er[None] * kk)
        return mag * jnp.cos(ei[None] * kk), mag * jnp.sin(ei[None] * kk)

    l1r, l1i = lam_pow(jnp.arange(1, 2))
    nr, ni = l1r[0] - 1.0, l1i[0]
    den = a_re * a_re + a_im * a_im
    fr, fi = (nr * a_re + ni * a_im) / den, (ni * a_re - nr * a_im) / den
    b_re, b_im = b_re.astype(F32), b_im.astype(F32)
    bbr = fr[:, :, None] * b_re - fi[:, :, None] * b_im
    bbi = fr[:, :, None] * b_im + fi[:, :, None] * b_re
    c_re, c_im = c_re.astype(F32), c_im.astype(F32)

    L = S5_CHUNK
    pr, pi = lam_pow(jnp.arange(0, L + 1))
    lbr = pr[:L, :, :, None] * bbr[None] - pi[:L, :, :, None] * bbi[None]
    lbi = pr[:L, :, :, None] * bbi[None] + pi[:L, :, :, None] * bbr[None]
    kern = (jnp.einsum('gpn,lgnq->lgqp', c_re, lbr, precision=hp)
            - jnp.einsum('gpn,lgnq->lgqp', c_im, lbi, precision=hp))
    j = jnp.arange(L)[:, None]
    t = jnp.arange(L)[None, :]
    lag = jnp.clip(t - j, 0, L - 1)
    m = jnp.where((t >= j)[:, :, None, None, None], kern[lag], 0.0)
    m = m.transpose(2, 0, 3, 1, 4).reshape(S5_GROUPS, S5_ROW, S5_ROW)
    rev = L - 1 - jnp.arange(L)
    wxr = lbr[rev].transpose(1, 0, 3, 2).reshape(S5_GROUPS, S5_ROW, S5_STATE)
    wxi = lbi[rev].transpose(1, 0, 3, 2).reshape(S5_GROUPS, S5_ROW, S5_STATE)
    wx = jnp.concatenate([wxr, wxi], axis=-1)
    wxs = jnp.concatenate([wxi, wxr], axis=-1)
    p1r, p1i = pr[1:], pi[1:]
    cr = c_re.transpose(0, 2, 1)[None]
    ci = c_im.transpose(0, 2, 1)[None]
    vr = cr * p1r[..., None] - ci * p1i[..., None]
    vi = cr * p1i[..., None] + ci * p1r[..., None]
    vo = jnp.concatenate([vr, -vi], axis=2).transpose(1, 2, 0, 3).reshape(S5_GROUPS, 2 * S5_STATE, S5_ROW)
    a1 = jnp.concatenate([pr[L], pr[L]], axis=-1)[:, None, :]
    a2 = jnp.concatenate([-pi[L], pi[L]], axis=-1)[:, None, :]
    d = jnp.tile(d_skip.astype(F32).reshape(S5_GROUPS, 1, S5_GROUP), (1, 1, L))
    eye = jnp.eye(L, dtype=F32)
    kron = lambda w: jnp.einsum('jt,gpq->gjptq', eye, w.astype(F32)).reshape(S5_GROUPS, S5_ROW, S5_ROW)
    return (m.astype(BF16), wx.astype(BF16), wxs.astype(BF16), vo.astype(BF16), a1, a2, d,
            kron(glu_val).astype(BF16), kron(glu_gate).astype(BF16))


def _proj_residual_kernel(*refs, n_in):
    x_ref = refs[0]
    o_ref = refs[1 + 2 * n_in]
    acc = x_ref[...]
    for i in range(n_in):
        acc = acc + _dot(refs[1 + i][...], refs[1 + n_in + i][...])
    o_ref[...] = acc


def _proj_residual(x, acts, weights):
    s = x.shape[0]
    tm = min(512, s)
    n_in = len(acts)
    return pl.pallas_call(
        functools.partial(_proj_residual_kernel, n_in=n_in),
        grid=(s // tm,),
        in_specs=([pl.BlockSpec((tm, D_MODEL), lambda i: (i, 0))]
                  + [pl.BlockSpec((tm, a.shape[1]), lambda i: (i, 0)) for a in acts]
                  + [pl.BlockSpec(w.shape, lambda i: (0, 0)) for w in weights]),
        out_specs=pl.BlockSpec((tm, D_MODEL), lambda i: (i, 0)),
        out_shape=jax.ShapeDtypeStruct((s, D_MODEL), F32),
        compiler_params=_cparams("parallel"),
        name="proj_residual",
    )(x, *acts, *weights)


def _mem_kv_kernel(mem_ref, g_ref, w_ref, kn_ref, k_ref, v_ref):
    kv = _dot(_rms(mem_ref[...], g_ref[...]).astype(BF16), w_ref[...])
    width = XA_HEADS * XA_HD
    for h in range(XA_HEADS):
        k_ref[h] = _rms(kv[:, h * XA_HD:(h + 1) * XA_HD], kn_ref[...]).astype(BF16)
        v_ref[h] = kv[:, width + h * XA_HD:width + (h + 1) * XA_HD].astype(BF16)


def _mem_kv(mem, g, w, kn):
    m_len = mem.shape[0]
    shape = jax.ShapeDtypeStruct((XA_HEADS, m_len, XA_HD), BF16)
    return pl.pallas_call(
        _mem_kv_kernel,
        out_shape=[shape, shape],
        compiler_params=pltpu.CompilerParams(vmem_limit_bytes=VMEM_LIMIT_BYTES),
        name="xa_mem_kv",
    )(mem, g, w, kn)


def _xa_kernel(x_ref, g_ref, wq_ref, qn_ref, k_ref, v_ref, wo_ref, o_ref):
    x = x_ref[...]
    q = _dot(_rms(x, g_ref[...]).astype(BF16), wq_ref[...])
    qn = qn_ref[...] * (XA_HD ** -0.5)
    heads = []
    for h in range(XA_HEADS):
        qh = _rms(q[:, h * XA_HD:(h + 1) * XA_HD], qn).astype(BF16)
        sc = _dot_nt(qh, k_ref[h])
        sc = sc - jnp.max(sc, axis=-1, keepdims=True)
        p = jnp.exp(sc)
        p = p / jnp.sum(p, axis=-1, keepdims=True)
        heads.append(_dot(p.astype(BF16), v_ref[h]).astype(BF16))
    o = jnp.concatenate(heads, axis=-1)
    o_ref[...] = x + _dot(o, wo_ref[...])


def _xa(x, g, wq, qn, k, v, wo):
    s = x.shape[0]
    tm = min(512, s)
    full = lambda a: pl.BlockSpec(a.shape, lambda i: (0,) * a.ndim)
    return pl.pallas_call(
        _xa_kernel,
        grid=(s // tm,),
        in_specs=[pl.BlockSpec((tm, D_MODEL), lambda i: (i, 0)), full(g), full(wq), full(qn),
                  full(k), full(v), full(wo)],
        out_specs=pl.BlockSpec((tm, D_MODEL), lambda i: (i, 0)),
        out_shape=jax.ShapeDtypeStruct((s, D_MODEL), F32),
        compiler_params=_cparams("parallel"),
        name="xa",
    )(x, g, wq, qn, k, v, wo)


def _mlp_kernel(x_ref, g_ref, wu_ref, wd_ref, o_ref, xn_ref):
    j = pl.program_id(1)

    @pl.when(j == 0)
    def _():
        x = x_ref[...]
        xn_ref[...] = _rms(x, g_ref[...]).astype(BF16)
        o_ref[...] = x

    hid = jnp.maximum(_dot(xn_ref[...], wu_ref[...]), 0.0)
    o_ref[...] += _dot((hid * hid).astype(BF16), wd_ref[...])


def _mlp(x, g, wu, wd):
    s = x.shape[0]
    tm = min(512, s)
    tf = 512
    return pl.pallas_call(
        _mlp_kernel,
        grid=(s // tm, D_FF // tf),
        in_specs=[pl.BlockSpec((tm, D_MODEL), lambda i, j: (i, 0)),
                  pl.BlockSpec((1, D_MODEL), lambda i, j: (0, 0)),
                  pl.BlockSpec((D_MODEL, tf), lambda i, j: (0, j)),
                  pl.BlockSpec((tf, D_MODEL), lambda i, j: (j, 0))],
        out_specs=pl.BlockSpec((tm, D_MODEL), lambda i, j: (i, 0)),
        out_shape=jax.ShapeDtypeStruct((s, D_MODEL), F32),
        scratch_shapes=[pltpu.VMEM((tm, D_MODEL), BF16)],
        compiler_params=_cparams("parallel", "arbitrary"),
        name="mlp",
    )(x, g, wu, wd)


def _moba_qkv_kernel(x_ref, g_ref, w_ref, hn_ref, o_ref, xn_ref, *, heads_per_block, norm_blocks):
    j = pl.program_id(1)

    @pl.when(j == 0)
    def _():
        xn_ref[...] = _rms(x_ref[...], g_ref[...]).astype(BF16)

    p = _dot(xn_ref[...], w_ref[...])

    @pl.when(j < norm_blocks)
    def _():
        for h in range(heads_per_block):
            sl = slice(h * MOBA_HD, (h + 1) * MOBA_HD)
            o_ref[:, sl] = _rms(p[:, sl], hn_ref[:, sl]).astype(o_ref.dtype)

    @pl.when(j >= norm_blocks)
    def _():
        o_ref[...] = p.astype(o_ref.dtype)


def _moba_qkv(x, g, w, head_gains):
    s = x.shape[0]
    tm = min(1024, s)
    tn = 512
    n = w.shape[1]
    kernel = functools.partial(_moba_qkv_kernel, heads_per_block=tn // MOBA_HD,
                               norm_blocks=2 * MOBA_HEADS * MOBA_HD // tn)
    return pl.pallas_call(
        kernel,
        grid=(s // tm, n // tn),
        in_specs=[pl.BlockSpec((tm, D_MODEL), lambda i, j: (i, 0)),
                  pl.BlockSpec((1, D_MODEL), lambda i, j: (0, 0)),
                  pl.BlockSpec((D_MODEL, tn), lambda i, j: (0, j)),
                  pl.BlockSpec((1, tn), lambda i, j: (0, j))],
        out_specs=pl.BlockSpec((tm, tn), lambda i, j: (i, j)),
        out_shape=jax.ShapeDtypeStruct((s, n), BF16),
        scratch_shapes=[pltpu.VMEM((tm, D_MODEL), BF16)],
        compiler_params=_cparams("parallel", "arbitrary"),
        name="moba_qkv",
    )(x, g, w, head_gains)


def _kmean_kernel(k_ref, o_ref, *, blocks):
    for b in range(blocks):
        kb = k_ref[b * MOBA_BLOCK:(b + 1) * MOBA_BLOCK, :].astype(F32)
        o_ref[b:b + 1, :] = jnp.mean(kb, axis=0, keepdims=True)


def _kmean(qkv):
    s = qkv.shape[0]
    width = MOBA_HEADS * MOBA_HD
    n_blk = s // MOBA_BLOCK
    blocks = 8 if n_blk % 8 == 0 else n_blk
    rows = blocks * MOBA_BLOCK
    return pl.pallas_call(
        functools.partial(_kmean_kernel, blocks=blocks),
        grid=(s // rows, width // 512),
        in_specs=[pl.BlockSpec((rows, 512), lambda i, j: (i, width // 512 + j))],
        out_specs=pl.BlockSpec((blocks, 512), lambda i, j: (i, j)),
        out_shape=jax.ShapeDtypeStruct((n_blk, width), F32),
        compiler_params=_cparams("parallel", "parallel"),
        name="moba_kmean",
    )(qkv)


def _moba_route_kernel(q_ref, k_ref, km_ref, qa_ref, ka_ref):
    i = pl.program_id(1)
    q = q_ref[...]
    gate = lax.dot_general(q.astype(F32), km_ref[...], (((1,), (1,)), ((), ())),
                           precision=lax.Precision.HIGHEST, preferred_element_type=F32)
    lane = lax.broadcasted_iota(jnp.int32, gate.shape, 1)
    lane_f = lane.astype(F32)
    gate = jnp.where(lane < i, gate, -jnp.inf)
    chosen = lane == i
    for r in range(MOBA_TOPK):
        best = jnp.max(gate, axis=-1, keepdims=True)
        idx = jnp.min(jnp.where(gate == best, lane_f, float(LANES)), axis=-1, keepdims=True)
        pick = lane_f == idx
        chosen = jnp.logical_or(chosen, jnp.logical_and(pick, r < i))
        gate = jnp.where(pick, -jnp.inf, gate)
    route = jnp.where(jnp.logical_and(chosen, lane < 64), 0.0, MASK_VALUE)
    route = jnp.where(lane < 64, route, 0.0)
    qa_ref[:, 0:128] = q
    qa_ref[:, 128:256] = route.astype(BF16)
    ka_ref[:, 0:128] = k_ref[...]
    ka_ref[:, 128:256] = jnp.where(lane == i, 1.0, 0.0).astype(BF16)


def _moba_route(qkv, kmean_t):
    s = qkv.shape[0]
    n_blk = s // MOBA_BLOCK
    aug = jax.ShapeDtypeStruct((MOBA_HEADS, s, HEAD_PAD), BF16)
    return pl.pallas_call(
        _moba_route_kernel,
        grid=(MOBA_HEADS, n_blk),
        in_specs=[pl.BlockSpec((MOBA_BLOCK, MOBA_HD), lambda h, i: (i, h)),
                  pl.BlockSpec((MOBA_BLOCK, MOBA_HD), lambda h, i: (i, MOBA_HEADS + h)),
                  pl.BlockSpec((None, LANES, MOBA_HD), lambda h, i: (h, 0, 0))],
        out_specs=[pl.BlockSpec((None, MOBA_BLOCK, HEAD_PAD), lambda h, i: (h, i, 0))] * 2,
        out_shape=[aug, aug],
        compiler_params=_cparams("parallel", "parallel"),
        name="moba_route",
    )(qkv, qkv, kmean_t)


def _rope_tables(s):
    inv_freq = 1.0 / (ROPE_BASE ** (jnp.arange(0, MLA_ROPE, 2, dtype=F32) / MLA_ROPE))
    ang = jnp.arange(s, dtype=F32)[:, None] * inv_freq[None, :]
    cos, sin = jnp.cos(ang), jnp.sin(ang)
    zero = jnp.zeros((s, 64), F32)
    return jnp.concatenate([cos, cos, zero], axis=1), jnp.concatenate([-sin, sin, zero], axis=1)


def _rope_cols(w):
    x1, x2 = w[..., :32], w[..., 32:]
    zero = jnp.zeros(w.shape[:-1] + (64,), w.dtype)
    return jnp.concatenate([x1, x2, zero], axis=-1), jnp.concatenate([x2, x1, zero], axis=-1)


def _row(v):
    return v.astype(F32).reshape(1, -1)


def _qk_gain_rows(g):
    g = g.astype(F32)
    return jnp.stack([g[:MLA_NOPE], jnp.concatenate([g[MLA_NOPE:], jnp.zeros((64,), F32)])])


def _cross_attention_mlp(x, mem, xa_norm, xa_mem_norm, xa_w_q, xa_w_kv, xa_q_norm, xa_k_norm, xa_w_o,
                         mlp_norm, w_up, w_down):
    k, v = _mem_kv(mem, _row(xa_mem_norm), xa_w_kv.astype(BF16), _row(xa_k_norm))
    x = _xa(x, _row(xa_norm), xa_w_q.astype(BF16), _row(xa_q_norm), k, v, xa_w_o.astype(BF16))
    return _mlp(x, _row(mlp_norm), w_up.astype(BF16), w_down.astype(BF16))


def kernel(x, mem, l0_mix_norm, l0_w_in, l0_q_lora_norm, l0_w_uq, l0_kv_lora_norm, l0_w_ukv, l0_mla_q_norm, l0_mla_k_norm, l0_s5_a_re, l0_s5_a_im, l0_s5_log_dt, l0_s5_b_re, l0_s5_b_im, l0_s5_c_re, l0_s5_c_im, l0_s5_d, l0_s5_glu_val, l0_s5_glu_gate, l0_w_out, l0_xa_norm, l0_xa_mem_norm, l0_xa_w_q, l0_xa_w_kv, l0_xa_q_norm, l0_xa_k_norm, l0_xa_w_o, l0_mlp_norm, l0_w_up, l0_w_down, l1_mix_norm, l1_w_qkv, l1_q_norm, l1_k_norm, l1_w_o, l1_xa_norm, l1_xa_mem_norm, l1_xa_w_q, l1_xa_w_kv, l1_xa_q_norm, l1_xa_k_norm, l1_xa_w_o, l1_mlp_norm, l1_w_up, l1_w_down):
    batch, s, _ = x.shape
    assert batch == 1 and s % MOBA_BLOCK == 0
    x = x.reshape(s, D_MODEL)
    mem = mem.reshape(mem.shape[1], D_MODEL)
    tile = min(512, s)
    no_slopes = jnp.zeros((MLA_HEADS,), F32)

    w_cq = l0_w_in[:, :Q_LORA]
    w_ckv = l0_w_in[:, Q_LORA:Q_LORA + KV_LORA]
    w_kra, w_krb = _rope_cols(l0_w_in[:, Q_LORA + KV_LORA:Q_LORA + KV_LORA + MLA_ROPE])
    w_u = l0_w_in[:, Q_LORA + KV_LORA + MLA_ROPE:]
    w_in = jnp.concatenate([w_cq, w_ckv, w_kra, w_krb, w_u], axis=1).astype(BF16)
    cq, ckv, kra, krb, u = _l0_in(x, _row(l0_mix_norm), w_in)

    wuq = l0_w_uq.reshape(Q_LORA, MLA_HEADS, MLA_QK)
    wuq_a, wuq_b = _rope_cols(wuq[:, :, MLA_NOPE:])
    wuq = jnp.concatenate([wuq[:, :, :MLA_NOPE], wuq_a, wuq_b], axis=-1).reshape(Q_LORA, MLA_HEADS * 384)
    cos_t, sin_t = _rope_tables(s)
    q, k, v = _mla_qkv(cq, ckv, kra, krb, cos_t, sin_t, _row(l0_q_lora_norm), _row(l0_kv_lora_norm),
                       wuq.astype(BF16), l0_w_ukv.astype(BF16),
                       _qk_gain_rows(l0_mla_q_norm), _qk_gain_rows(l0_mla_k_norm))
    kb, vt = _blocked_kv(k, v, tile)
    y_a = _flash(q, kb, vt, no_slopes, tile=tile, alibi=False)

    n_chunks = s // S5_CHUNK
    ops = _s5_operators(l0_s5_a_re, l0_s5_a_im, l0_s5_log_dt, l0_s5_b_re, l0_s5_b_im,
                        l0_s5_c_re, l0_s5_c_im, l0_s5_d, l0_s5_glu_val, l0_s5_glu_gate)
    u_rows = u.reshape(n_chunks, S5_CHUNK, S5_GROUPS, S5_GROUP).transpose(2, 0, 1, 3)
    u_rows = u_rows.reshape(S5_GROUPS, n_chunks, S5_ROW)
    y_rows = _s5(u_rows, *ops)
    y_b = y_rows.reshape(S5_GROUPS, n_chunks, S5_CHUNK, S5_GROUP).transpose(1, 2, 0, 3).reshape(s, S5_WIDTH)

    w_out = l0_w_out.astype(BF16)
    x = _proj_residual(x, [y_a, y_b], [w_out[:MLA_HEADS * MLA_V], w_out[MLA_HEADS * MLA_V:]])
    x = _cross_attention_mlp(x, mem, l0_xa_norm, l0_xa_mem_norm, l0_xa_w_q, l0_xa_w_kv, l0_xa_q_norm,
                             l0_xa_k_norm, l0_xa_w_o, l0_mlp_norm, l0_w_up, l0_w_down)

    width = MOBA_HEADS * MOBA_HD
    head_gains = jnp.concatenate([jnp.tile(l1_q_norm.astype(F32) * MOBA_HD ** -0.5, MOBA_HEADS),
                                  jnp.tile(l1_k_norm.astype(F32), MOBA_HEADS),
                                  jnp.ones((width,), F32)]).reshape(1, 3 * width)
    qkv = _moba_qkv(x, _row(l1_mix_norm), l1_w_qkv.astype(BF16), head_gains)
    kmean = _kmean(qkv)
    n_blk = s // MOBA_BLOCK
    kmean_t = kmean.reshape(n_blk, MOBA_HEADS, MOBA_HD).transpose(1, 0, 2)
    kmean_t = jnp.pad(kmean_t, ((0, 0), (0, LANES - n_blk), (0, 0)))
    q_aug, k_aug = _moba_route(qkv, kmean_t)
    v1 = qkv[:, 2 * width:].reshape(s, MOBA_HEADS, MOBA_HD).transpose(1, 0, 2)
    kb1, vt1 = _blocked_kv(k_aug, v1, tile)
    slopes = 2.0 ** (-8.0 * jnp.arange(1, MOBA_HEADS + 1, dtype=F32) / MOBA_HEADS)
    o1 = _flash(q_aug, kb1, vt1, slopes, tile=tile, alibi=True)
    x = _proj_residual(x, [o1], [l1_w_o.astype(BF16)])
    x = _cross_attention_mlp(x, mem, l1_xa_norm, l1_xa_mem_norm, l1_xa_w_q, l1_xa_w_kv, l1_xa_q_norm,
                             l1_xa_k_norm, l1_xa_w_o, l1_mlp_norm, l1_w_up, l1_w_down)
    return x.reshape(1, s, D_MODEL)
```

```python
import functools
import math

import jax
import jax.numpy as jnp
from jax import lax
from jax.experimental import pallas as pl
from jax.experimental.pallas import tpu as pltpu

F32 = jnp.float32
BF16 = jnp.bfloat16

D_MODEL = 2048
EPS = 1e-6
MASK_VALUE = -1e30

MLA_HEADS = 8
MLA_NOPE = 128
MLA_ROPE = 64
MLA_QK = MLA_NOPE + MLA_ROPE
MLA_V = 128
Q_LORA = 512
KV_LORA = 512
ROPE_BASE = 10000.0
S5_WIDTH = 1024
S5_GROUP = 16
S5_GROUPS = 64
S5_STATE = 64
S5_CHUNK = 16
S5_OCTET = 8
S5_CHUNK_BLOCK = 256
MOBA_HEADS = 16
MOBA_HD = 128
MOBA_BLOCK = 256
MOBA_TOPK = 3
XA_HEADS = 4
XA_HD = 128
D_FF = 4 * D_MODEL

LANES = 128
LOG2E = math.log2(math.e)
FLASH_TILE = 1024
HEAD_PAD = 256
VT_ROWS = 136
VMEM_LIMIT_BYTES = 56 * 1024 * 1024


def _cparams(*sem):
    return pltpu.CompilerParams(dimension_semantics=sem, vmem_limit_bytes=VMEM_LIMIT_BYTES)


def _rms(x, g):
    return x * lax.rsqrt(jnp.mean(x * x, axis=-1, keepdims=True) + EPS) * g


def _dot(a, b):
    return jnp.dot(a, b, preferred_element_type=F32)


def _dot_nt(a, b):
    return lax.dot_general(a, b, (((1,), (1,)), ((), ())), preferred_element_type=F32)


def _l0_in_kernel(x_ref, g_ref, w_ref, cq_ref, ckv_ref, kra_ref, krb_ref, u_ref):
    xn = _rms(x_ref[...], g_ref[...]).astype(BF16)
    p = _dot(xn, w_ref[...])
    cq_ref[...] = p[:, 0:512]
    ckv_ref[...] = p[:, 512:1024]
    kra_ref[...] = p[:, 1024:1152]
    krb_ref[...] = p[:, 1152:1280]
    u_ref[...] = p[:, 1280:2304]


def _l0_in(x, g, w):
    s = x.shape[0]
    tm = min(512, s)
    n = w.shape[1]
    row = lambda width: pl.BlockSpec((tm, width), lambda i: (i, 0))
    return pl.pallas_call(
        _l0_in_kernel,
        grid=(s // tm,),
        in_specs=[row(D_MODEL), pl.BlockSpec((1, D_MODEL), lambda i: (0, 0)),
                  pl.BlockSpec((D_MODEL, n), lambda i: (0, 0))],
        out_specs=[row(512), row(512), row(128), row(128), row(1024)],
        out_shape=[jax.ShapeDtypeStruct((s, 512), F32), jax.ShapeDtypeStruct((s, 512), F32),
                   jax.ShapeDtypeStruct((s, 128), F32), jax.ShapeDtypeStruct((s, 128), F32),
                   jax.ShapeDtypeStruct((s, 1024), F32)],
        compiler_params=_cparams("parallel"),
        name="l0_in_proj",
    )(x, g, w)


def _mla_qkv_kernel(cq_ref, ckv_ref, kra_ref, krb_ref, cos_ref, sin_ref, gq_ref, gkv_ref,
                    wuq_ref, wukv_ref, qn_ref, kn_ref, q_ref, k_ref, v_ref):
    cos = cos_ref[...]
    sin = sin_ref[...]
    qn = qn_ref[...]
    kn = kn_ref[...]
    qp = _dot(_rms(cq_ref[...], gq_ref[...]).astype(BF16), wuq_ref[...])
    kvp = _dot(_rms(ckv_ref[...], gkv_ref[...]).astype(BF16), wukv_ref[...])
    k_rot = kra_ref[...] * cos + krb_ref[...] * sin
    k_rot_ss = jnp.sum(k_rot * k_rot, axis=-1, keepdims=True)
    q_scale = MLA_QK ** -0.5 * LOG2E
    for h in range(MLA_HEADS):
        qb = h * 384
        nope = qp[:, qb:qb + 128]
        rot = qp[:, qb + 128:qb + 256] * cos + qp[:, qb + 256:qb + 384] * sin
        ss = jnp.sum(nope * nope, axis=-1, keepdims=True) + jnp.sum(rot * rot, axis=-1, keepdims=True)
        inv = lax.rsqrt(ss * (1.0 / MLA_QK) + EPS) * q_scale
        q_ref[h, :, 0:128] = (nope * inv * qn[0:1, :]).astype(BF16)
        q_ref[h, :, 128:256] = (rot * inv * qn[1:2, :]).astype(BF16)
        kb = h * 256
        k_nope = kvp[:, kb:kb + 128]
        ssk = jnp.sum(k_nope * k_nope, axis=-1, keepdims=True) + k_rot_ss
        invk = lax.rsqrt(ssk * (1.0 / MLA_QK) + EPS)
        k_ref[h, :, 0:128] = (k_nope * invk * kn[0:1, :]).astype(BF16)
        k_ref[h, :, 128:256] = (k_rot * invk * kn[1:2, :]).astype(BF16)
        v_ref[h] = kvp[:, kb + 128:kb + 256].astype(BF16)


def _mla_qkv(cq, ckv, kra, krb, cos_t, sin_t, gq, gkv, wuq, wukv, qn, kn):
    s = cq.shape[0]
    tm = min(256, s)
    row = lambda width: pl.BlockSpec((tm, width), lambda i: (i, 0))
    full = lambda a: pl.BlockSpec(a.shape, lambda i: (0,) * a.ndim)
    head_out = lambda width: pl.BlockSpec((MLA_HEADS, tm, width), lambda i: (0, i, 0))
    return pl.pallas_call(
        _mla_qkv_kernel,
        grid=(s // tm,),
        in_specs=[row(512), row(512), row(128), row(128), row(128), row(128),
                  full(gq), full(gkv), full(wuq), full(wukv), full(qn), full(kn)],
        out_specs=[head_out(HEAD_PAD), head_out(HEAD_PAD), head_out(MLA_V)],
        out_shape=[jax.ShapeDtypeStruct((MLA_HEADS, s, HEAD_PAD), BF16),
                   jax.ShapeDtypeStruct((MLA_HEADS, s, HEAD_PAD), BF16),
                   jax.ShapeDtypeStruct((MLA_HEADS, s, MLA_V), BF16)],
        compiler_params=_cparams("parallel"),
        name="mla_qkv",
    )(cq, ckv, kra, krb, cos_t, sin_t, gq, gkv, wuq, wukv, qn, kn)


def _flash_kernel(slope_ref, q_ref, k_ref, vt_ref, o_ref, m_ref, acc_ref, s_ref, cm_ref, *, tile):
    h = pl.program_id(0)
    pair = pl.program_id(1)
    base = 2 * pair
    slope_tile = slope_ref[h] * tile
    m_ref[...] = jnp.full(m_ref.shape, -jnp.inf, F32)
    acc_ref[...] = jnp.zeros(acc_ref.shape, F32)

    def scores(c, kj, diagonal):
        st = _dot_nt(k_ref[kj], q_ref[c * tile:(c + 1) * tile, :])
        if diagonal:
            key_idx = lax.broadcasted_iota(jnp.int32, (tile, tile), 0)
            qry_idx = lax.broadcasted_iota(jnp.int32, (tile, tile), 1)
            st = jnp.where(key_idx <= qry_idx, st, MASK_VALUE)
        return st, jnp.max(st, axis=0, keepdims=True)

    def softmax_pv(c, kj, st, col_max):
        shift = slope_tile * (kj - (base + c)).astype(F32)
        m_prev = m_ref[c]
        m_new = jnp.maximum(m_prev, col_max + shift)
        alpha = jnp.exp2(m_prev - m_new)
        p = jnp.exp2(st - (m_new - shift)).astype(BF16)
        acc_ref[c] = alpha * acc_ref[c] + _dot(vt_ref[kj], p)
        m_ref[c] = m_new

    softmax_pv(1, base, *scores(1, base, False))
    for c in range(2):
        st, col_max = scores(c, base + c, True)
        s_ref[c] = st
        cm_ref[c] = col_max

    def held_block(c, item):
        return jnp.where(item == 0, base + c, item - 1)

    def body(t, carry):
        i0 = 2 * t
        first = [scores(c, i0, False) for c in range(2)]
        for c in range(2):
            softmax_pv(c, held_block(c, i0), s_ref[c], cm_ref[c])
        second = [scores(c, i0 + 1, False) for c in range(2)]
        for c in range(2):
            softmax_pv(c, i0, *first[c])
        for c in range(2):
            s_ref[c] = second[c][0]
            cm_ref[c] = second[c][1]
        return carry

    lax.fori_loop(0, pair, body, 0)
    for c in range(2):
        softmax_pv(c, held_block(c, base), s_ref[c], cm_ref[c])
        acc = acc_ref[c]
        out_t = acc[0:128, :] / acc[128:129, :]
        o_ref[c * tile:(c + 1) * tile, :] = out_t.T.astype(o_ref.dtype)


def _flash(q, k, vt, slopes, *, tile, name):
    heads, s, _ = q.shape
    nk = s // tile
    return pl.pallas_call(
        functools.partial(_flash_kernel, tile=tile),
        grid_spec=pltpu.PrefetchScalarGridSpec(
            num_scalar_prefetch=1,
            grid=(heads, s // (2 * tile)),
            in_specs=[pl.BlockSpec((None, 2 * tile, HEAD_PAD), lambda h, i, sl: (h, i, 0)),
                      pl.BlockSpec((None, nk, tile, HEAD_PAD), lambda h, i, sl: (h, 0, 0, 0)),
                      pl.BlockSpec((None, nk, VT_ROWS, tile), lambda h, i, sl: (h, 0, 0, 0))],
            out_specs=pl.BlockSpec((2 * tile, 128), lambda h, i, sl: (i, h)),
            scratch_shapes=[pltpu.VMEM((2, 1, tile), F32), pltpu.VMEM((2, VT_ROWS, tile), F32),
                            pltpu.VMEM((2, tile, tile), F32), pltpu.VMEM((2, 1, tile), F32)],
        ),
        out_shape=jax.ShapeDtypeStruct((s, heads * 128), BF16),
        compiler_params=_cparams("parallel", "arbitrary"),
        name=name,
    )(slopes, q, k, vt)


def _blocked_kv(k, v, tile):
    heads, s, _ = k.shape
    nk = s // tile
    kb = k.reshape(heads, nk, tile, HEAD_PAD)
    vt = jnp.swapaxes(v.reshape(heads, nk, tile, 128), 2, 3)
    ones = jnp.ones((heads, nk, 1, tile), v.dtype)
    zeros = jnp.zeros((heads, nk, VT_ROWS - 129, tile), v.dtype)
    return kb, jnp.concatenate([vt, ones, zeros], axis=2)


def _s5_kernel(u_ref, m_ref, wx_ref, vo_ref, a_ref, d_ref, glu_ref, o_ref, carry_ref, x_ref, s_ref):
    cb = pl.program_id(1)
    n_chunks = u_ref.shape[0]
    half = S5_OCTET * S5_STATE

    @pl.when(cb == 0)
    def _():
        carry_ref[...] = jnp.zeros(carry_ref.shape, F32)

    steps = [u_ref[:, j, :] for j in range(S5_CHUNK)]
    xb = jnp.concatenate([p.astype(BF16) for p in steps], axis=1)
    x_ref[...] = _dot(xb, wx_ref[...])
    a_re = a_ref[0:1, :]
    a_im = a_ref[1:2, :]

    def body(c, carry):
        re, im = carry
        s_ref[pl.ds(c, 1), :] = jnp.concatenate([re, im], axis=1)
        x = x_ref[pl.ds(c, 1), :]
        return (a_re * re - a_im * im + x[:, :half], a_re * im + a_im * re + x[:, half:])

    re, im = lax.fori_loop(0, n_chunks, body, (carry_ref[0:1, :], carry_ref[1:2, :]), unroll=8)
    carry_ref[0:1, :] = re
    carry_ref[1:2, :] = im

    sb = s_ref[...].astype(BF16)
    d = d_ref[...]
    for tp in range(S5_CHUNK // 2):
        lo, hi = 2 * tp * LANES, (2 * tp + 2) * LANES
        y2 = _dot(xb[:, :hi], m_ref[0:hi, lo:hi]) + _dot(sb, vo_ref[:, lo:hi])
        ys = [jax.nn.gelu(y2[:, k * LANES:(k + 1) * LANES] + d * steps[2 * tp + k]).astype(BF16)
              for k in range(2)]
        vg = _dot(jnp.concatenate(ys, axis=0), glu_ref[...])
        out = vg[:, :LANES] * jax.nn.sigmoid(vg[:, LANES:])
        for k in range(2):
            o_ref[:, 2 * tp + k, :] = out[k * n_chunks:(k + 1) * n_chunks, :]


def _s5(u, m, wx, vo, a, d, glu):
    s = u.shape[0]
    n_chunks = s // S5_CHUNK
    cb = min(S5_CHUNK_BLOCK, n_chunks)
    octets = S5_GROUPS // S5_OCTET
    u3 = u.reshape(n_chunks, S5_CHUNK, S5_WIDTH)
    per_octet = lambda arr: pl.BlockSpec((None,) + arr.shape[1:], lambda o, i: (o,) + (0,) * (arr.ndim - 1),
                                         pipeline_mode=pl.Buffered(1))
    io_spec = pl.BlockSpec((cb, S5_CHUNK, LANES), lambda o, i: (i, 0, o))
    state_w = 2 * S5_OCTET * S5_STATE
    y3 = pl.pallas_call(
        _s5_kernel,
        grid=(octets, n_chunks // cb),
        in_specs=[io_spec] + [per_octet(arr) for arr in (m, wx, vo, a, d, glu)],
        out_specs=io_spec,
        out_shape=jax.ShapeDtypeStruct((n_chunks, S5_CHUNK, S5_WIDTH), F32),
        scratch_shapes=[pltpu.VMEM((2, state_w // 2), F32), pltpu.VMEM((cb, state_w), F32),
                        pltpu.VMEM((cb, state_w), F32)],
        compiler_params=_cparams("parallel", "arbitrary"),
        name="s5_scan",
    )(u3, m, wx, vo, a, d, glu)
    return y3.reshape(s, S5_WIDTH)


def _s5_operators(a_re, a_im, log_dt, b_re, b_im, c_re, c_im, d_skip, glu_val, glu_gate):
    hp = lax.Precision.HIGHEST
    a_re, a_im = a_re.astype(F32), a_im.astype(F32)
    dt = jnp.exp(log_dt.astype(F32))[:, None]
    er, ei = a_re * dt, a_im * dt

    def lam_pow(k):
        kk = k.astype(F32)[:, None, None]
        mag = jnp.exp(er[None] * kk)
        return mag * jnp.cos(ei[None] * kk), mag * jnp.sin(ei[None] * kk)

    l1r, l1i = lam_pow(jnp.arange(1, 2))
    nr, ni = l1r[0] - 1.0, l1i[0]
    den = a_re * a_re + a_im * a_im
    fr, fi = (nr * a_re + ni * a_im) / den, (ni * a_re - nr * a_im) / den
    b_re, b_im = b_re.astype(F32), b_im.astype(F32)
    bbr = fr[:, :, None] * b_re - fi[:, :, None] * b_im
    bbi = fr[:, :, None] * b_im + fi[:, :, None] * b_re
    c_re, c_im = c_re.astype(F32), c_im.astype(F32)

    L, A, N, P = S5_CHUNK, S5_OCTET, S5_STATE, S5_GROUP
    O = S5_GROUPS // A
    same = jnp.eye(A, dtype=bool)
    pr, pi = lam_pow(jnp.arange(0, L + 1))
    lbr = pr[:L, :, :, None] * bbr[None] - pi[:L, :, :, None] * bbi[None]
    lbi = pr[:L, :, :, None] * bbi[None] + pi[:L, :, :, None] * bbr[None]
    kern = (jnp.einsum('gpn,lgnq->lgqp', c_re, lbr, precision=hp)
            - jnp.einsum('gpn,lgnq->lgqp', c_im, lbi, precision=hp))
    j = jnp.arange(L)[:, None]
    t = jnp.arange(L)[None, :]
    lag = jnp.clip(t - j, 0, L - 1)
    kl = kern[lag].reshape(L, L, O, A, P, P).transpose(2, 0, 3, 4, 1, 5)
    keep = (t >= j)[None, :, None, None, :, None, None] & same[None, None, :, None, None, :, None]
    m = jnp.where(keep, kl[:, :, :, :, :, None, :], 0.0).astype(BF16)
    m = m.reshape(O, L * A * P, L * A * P)
    rev = L - 1 - jnp.arange(L)

    def state_cols(w):
        w = w.reshape(L, O, A, N, P).transpose(1, 0, 2, 4, 3)
        w = jnp.where(same[None, None, :, None, :, None], w[:, :, :, :, None, :], 0.0)
        return w.astype(BF16).reshape(O, L * A * P, A * N)

    wx = jnp.concatenate([state_cols(lbr[rev]), state_cols(lbi[rev])], axis=-1)
    p1r, p1i = pr[1:], pi[1:]
    cr = c_re.transpose(0, 2, 1)[None]
    ci = c_im.transpose(0, 2, 1)[None]
    vr = cr * p1r[..., None] - ci * p1i[..., None]
    vi = cr * p1i[..., None] + ci * p1r[..., None]

    def state_rows(v):
        v = v.reshape(L, O, A, N, P).transpose(1, 2, 3, 0, 4)
        v = jnp.where(same[None, :, None, None, :, None], v[:, :, :, :, None, :], 0.0)
        return v.astype(BF16).reshape(O, A * N, L * A * P)

    vo = jnp.concatenate([state_rows(vr), state_rows(-vi)], axis=1)
    a = jnp.stack([pr[L].reshape(O, A * N), pi[L].reshape(O, A * N)], axis=1)
    d = d_skip.astype(F32).reshape(O, 1, A * P)

    def glu_cols(w):
        w = w.astype(F32).reshape(O, A, P, P)
        w = jnp.where(same[None, :, None, :, None], w[:, :, :, None, :], 0.0)
        return w.astype(BF16).reshape(O, A * P, A * P)

    glu = jnp.concatenate([glu_cols(glu_val), glu_cols(glu_gate)], axis=-1)
    return m, wx, vo, a, d, glu


def _proj_residual_kernel(*refs, n_in):
    x_ref = refs[0]
    o_ref = refs[1 + 2 * n_in]
    acc = x_ref[...]
    for i in range(n_in):
        acc = acc + _dot(refs[1 + i][...].astype(BF16), refs[1 + n_in + i][...])
    o_ref[...] = acc


def _proj_residual(x, acts, weights):
    s = x.shape[0]
    tm = min(512, s)
    n_in = len(acts)
    return pl.pallas_call(
        functools.partial(_proj_residual_kernel, n_in=n_in),
        grid=(s // tm,),
        in_specs=([pl.BlockSpec((tm, D_MODEL), lambda i: (i, 0))]
                  + [pl.BlockSpec((tm, a.shape[1]), lambda i: (i, 0)) for a in acts]
                  + [pl.BlockSpec(w.shape, lambda i: (0, 0)) for w in weights]),
        out_specs=pl.BlockSpec((tm, D_MODEL), lambda i: (i, 0)),
        out_shape=jax.ShapeDtypeStruct((s, D_MODEL), F32),
        compiler_params=_cparams("parallel"),
        name="proj_residual",
    )(x, *acts, *weights)


def _mem_kv_kernel(mem_ref, g_ref, w_ref, kn_ref, k_ref, v_ref):
    kv = _dot(_rms(mem_ref[...], g_ref[...]).astype(BF16), w_ref[...])
    width = XA_HEADS * XA_HD
    for h in range(XA_HEADS):
        k_ref[h] = _rms(kv[:, h * XA_HD:(h + 1) * XA_HD], kn_ref[...]).astype(BF16)
        v_ref[h] = kv[:, width + h * XA_HD:width + (h + 1) * XA_HD].astype(BF16)


def _mem_kv(mem, g, w, kn):
    m_len = mem.shape[0]
    shape = jax.ShapeDtypeStruct((XA_HEADS, m_len, XA_HD), BF16)
    return pl.pallas_call(
        _mem_kv_kernel,
        out_shape=[shape, shape],
        compiler_params=pltpu.CompilerParams(vmem_limit_bytes=VMEM_LIMIT_BYTES),
        name="xa_mem_kv",
    )(mem, g, w, kn)


def _xa_kernel(x_ref, g_ref, wq_ref, qn_ref, k_ref, v_ref, wo_ref, o_ref):
    x = x_ref[...]
    q = _dot(_rms(x, g_ref[...]).astype(BF16), wq_ref[...])
    qn = qn_ref[...] * (XA_HD ** -0.5)
    heads = []
    for h in range(XA_HEADS):
        qh = _rms(q[:, h * XA_HD:(h + 1) * XA_HD], qn).astype(BF16)
        sc = _dot_nt(qh, k_ref[h])
        sc = sc - jnp.max(sc, axis=-1, keepdims=True)
        p = jnp.exp(sc)
        p = p / jnp.sum(p, axis=-1, keepdims=True)
        heads.append(_dot(p.astype(BF16), v_ref[h]).astype(BF16))
    o = jnp.concatenate(heads, axis=-1)
    o_ref[...] = x + _dot(o, wo_ref[...])


def _xa(x, g, wq, qn, k, v, wo):
    s = x.shape[0]
    tm = min(512, s)
    full = lambda a: pl.BlockSpec(a.shape, lambda i: (0,) * a.ndim)
    return pl.pallas_call(
        _xa_kernel,
        grid=(s // tm,),
        in_specs=[pl.BlockSpec((tm, D_MODEL), lambda i: (i, 0)), full(g), full(wq), full(qn),
                  full(k), full(v), full(wo)],
        out_specs=pl.BlockSpec((tm, D_MODEL), lambda i: (i, 0)),
        out_shape=jax.ShapeDtypeStruct((s, D_MODEL), F32),
        compiler_params=_cparams("parallel"),
        name="xa",
    )(x, g, wq, qn, k, v, wo)


def _mlp_kernel(x_ref, g_ref, wu_ref, wd_ref, o_ref, xn_ref):
    j = pl.program_id(1)

    @pl.when(j == 0)
    def _():
        x = x_ref[...]
        xn_ref[...] = _rms(x, g_ref[...]).astype(BF16)
        o_ref[...] = x

    hid = jnp.maximum(_dot(xn_ref[...], wu_ref[...]), 0.0)
    o_ref[...] += _dot((hid * hid).astype(BF16), wd_ref[...])


def _mlp(x, g, wu, wd):
    s = x.shape[0]
    tm = min(512, s)
    tf = 512
    return pl.pallas_call(
        _mlp_kernel,
        grid=(s // tm, D_FF // tf),
        in_specs=[pl.BlockSpec((tm, D_MODEL), lambda i, j: (i, 0)),
                  pl.BlockSpec((1, D_MODEL), lambda i, j: (0, 0)),
                  pl.BlockSpec((D_MODEL, tf), lambda i, j: (0, j)),
                  pl.BlockSpec((tf, D_MODEL), lambda i, j: (j, 0))],
        out_specs=pl.BlockSpec((tm, D_MODEL), lambda i, j: (i, 0)),
        out_shape=jax.ShapeDtypeStruct((s, D_MODEL), F32),
        scratch_shapes=[pltpu.VMEM((tm, D_MODEL), BF16)],
        compiler_params=_cparams("parallel", "arbitrary"),
        name="mlp",
    )(x, g, wu, wd)


def _moba_qkv_kernel(x_ref, g_ref, w_ref, hn_ref, o_ref, xn_ref, *, heads_per_block, norm_blocks):
    j = pl.program_id(1)

    @pl.when(j == 0)
    def _():
        xn_ref[...] = _rms(x_ref[...], g_ref[...]).astype(BF16)

    p = _dot(xn_ref[...], w_ref[...])

    @pl.when(j < norm_blocks)
    def _():
        for h in range(heads_per_block):
            sl = slice(h * MOBA_HD, (h + 1) * MOBA_HD)
            o_ref[:, sl] = _rms(p[:, sl], hn_ref[:, sl]).astype(o_ref.dtype)

    @pl.when(j >= norm_blocks)
    def _():
        o_ref[...] = p.astype(o_ref.dtype)


def _moba_qkv(x, g, w, head_gains):
    s = x.shape[0]
    tm = min(1024, s)
    tn = 512
    n = w.shape[1]
    kernel = functools.partial(_moba_qkv_kernel, heads_per_block=tn // MOBA_HD,
                               norm_blocks=2 * MOBA_HEADS * MOBA_HD // tn)
    return pl.pallas_call(
        kernel,
        grid=(s // tm, n // tn),
        in_specs=[pl.BlockSpec((tm, D_MODEL), lambda i, j: (i, 0)),
                  pl.BlockSpec((1, D_MODEL), lambda i, j: (0, 0)),
                  pl.BlockSpec((D_MODEL, tn), lambda i, j: (0, j)),
                  pl.BlockSpec((1, tn), lambda i, j: (0, j))],
        out_specs=pl.BlockSpec((tm, tn), lambda i, j: (i, j)),
        out_shape=jax.ShapeDtypeStruct((s, n), BF16),
        scratch_shapes=[pltpu.VMEM((tm, D_MODEL), BF16)],
        compiler_params=_cparams("parallel", "arbitrary"),
        name="moba_qkv",
    )(x, g, w, head_gains)


def _kmean_kernel(k_ref, o_ref, *, blocks):
    for b in range(blocks):
        kb = k_ref[b * MOBA_BLOCK:(b + 1) * MOBA_BLOCK, :].astype(F32)
        o_ref[b:b + 1, :] = jnp.mean(kb, axis=0, keepdims=True)


def _kmean(qkv):
    s = qkv.shape[0]
    width = MOBA_HEADS * MOBA_HD
    n_blk = s // MOBA_BLOCK
    blocks = 8 if n_blk % 8 == 0 else n_blk
    rows = blocks * MOBA_BLOCK
    return pl.pallas_call(
        functools.partial(_kmean_kernel, blocks=blocks),
        grid=(s // rows, width // 512),
        in_specs=[pl.BlockSpec((rows, 512), lambda i, j: (i, width // 512 + j))],
        out_specs=pl.BlockSpec((blocks, 512), lambda i, j: (i, j)),
        out_shape=jax.ShapeDtypeStruct((n_blk, width), F32),
        compiler_params=_cparams("parallel", "parallel"),
        name="moba_kmean",
    )(qkv)


def _split3(x):
    hi = x.astype(BF16)
    r1 = x - hi.astype(F32)
    mid = r1.astype(BF16)
    lo = (r1 - mid.astype(F32)).astype(BF16)
    return hi.astype(F32), mid.astype(F32), lo.astype(F32)


ROUTE_LANES = 64
SLOPE_RC_LANE = 64
SLOPE_RB_LANE = 67
QUERY_BIAS_LANE = 70


def _moba_route_kernel(q_ref, k_ref, km_ref, slope_ref, slope_cols_ref, qa_ref, ka_ref, *, blocks_per_tile):
    i = pl.program_id(0)
    shape = (MOBA_BLOCK, LANES)
    lane = lax.broadcasted_iota(jnp.int32, shape, 1)
    lane_f = lane.astype(F32)
    row_f = lax.broadcasted_iota(jnp.int32, shape, 0).astype(F32)
    blk_in_tile = (i % blocks_per_tile).astype(F32)
    offset_in_tile = blk_in_tile * MOBA_BLOCK + row_f
    k_extra = jnp.where(lane == i, 1.0, 0.0)
    k_extra = jnp.where((lane >= SLOPE_RC_LANE) & (lane < SLOPE_RC_LANE + 3), row_f, k_extra)
    k_extra = jnp.where((lane >= SLOPE_RB_LANE) & (lane < SLOPE_RB_LANE + 3), blk_in_tile, k_extra)
    k_extra = jnp.where((lane >= QUERY_BIAS_LANE) & (lane < QUERY_BIAS_LANE + 3), 1.0, k_extra)
    k_extra = k_extra.astype(BF16)
    for h in range(MOBA_HEADS):
        sl = slice(h * MOBA_HD, (h + 1) * MOBA_HD)
        q = q_ref[:, sl]
        gate = lax.dot_general(q.astype(F32), km_ref[h], (((1,), (1,)), ((), ())),
                               precision=lax.Precision.HIGHEST, preferred_element_type=F32)
        gate = jnp.where(lane < i, gate, -jnp.inf)
        chosen = lane == i
        for r in range(MOBA_TOPK):
            best = jnp.max(gate, axis=-1, keepdims=True)
            idx = jnp.min(jnp.where(gate == best, lane_f, float(LANES)), axis=-1, keepdims=True)
            pick = lane_f == idx
            chosen = jnp.logical_or(chosen, jnp.logical_and(pick, r < i))
            gate = jnp.where(pick, -jnp.inf, gate)
        q_extra = jnp.where(chosen, 0.0, MASK_VALUE)
        q_extra = jnp.where(lane < ROUTE_LANES, q_extra, slope_cols_ref[h])
        bias = _split3(-slope_ref[h] * offset_in_tile)
        for piece in range(3):
            q_extra = jnp.where(lane == QUERY_BIAS_LANE + piece, bias[piece], q_extra)
        qa_ref[h, :, 0:128] = q
        qa_ref[h, :, 128:256] = q_extra.astype(BF16)
        ka_ref[h, :, 0:128] = k_ref[:, sl]
        ka_ref[h, :, 128:256] = k_extra


def _moba_route(qkv, kmean_t, slopes, tile):
    s = qkv.shape[0]
    n_blk = s // MOBA_BLOCK
    width = MOBA_HEADS * MOBA_HD
    aug = jax.ShapeDtypeStruct((MOBA_HEADS, s, HEAD_PAD), BF16)
    pieces = _split3(slopes)
    lane = jnp.arange(LANES)
    cols = jnp.zeros((MOBA_HEADS, LANES), F32)
    for piece in range(3):
        cols = jnp.where(lane[None, :] == SLOPE_RC_LANE + piece, pieces[piece][:, None], cols)
        cols = jnp.where(lane[None, :] == SLOPE_RB_LANE + piece, MOBA_BLOCK * pieces[piece][:, None], cols)
    slope_rows = jnp.broadcast_to(slopes[:, None, None], (MOBA_HEADS, 1, LANES))
    full = lambda a: pl.BlockSpec(a.shape, lambda i: (0,) * a.ndim)
    return pl.pallas_call(
        functools.partial(_moba_route_kernel, blocks_per_tile=tile // MOBA_BLOCK),
        grid=(n_blk,),
        in_specs=[pl.BlockSpec((MOBA_BLOCK, width), lambda i: (i, 0)),
                  pl.BlockSpec((MOBA_BLOCK, width), lambda i: (i, 1)),
                  full(kmean_t), full(slope_rows), full(cols[:, None, :])],
        out_specs=[pl.BlockSpec((MOBA_HEADS, MOBA_BLOCK, HEAD_PAD), lambda i: (0, i, 0))] * 2,
        out_shape=[aug, aug],
        compiler_params=_cparams("parallel"),
        name="moba_route",
    )(qkv, qkv, kmean_t, slope_rows, cols[:, None, :])


def _rope_tables(s):
    inv_freq = 1.0 / (ROPE_BASE ** (jnp.arange(0, MLA_ROPE, 2, dtype=F32) / MLA_ROPE))
    ang = jnp.arange(s, dtype=F32)[:, None] * inv_freq[None, :]
    cos, sin = jnp.cos(ang), jnp.sin(ang)
    zero = jnp.zeros((s, 64), F32)
    return jnp.concatenate([cos, cos, zero], axis=1), jnp.concatenate([-sin, sin, zero], axis=1)


def _rope_cols(w):
    x1, x2 = w[..., :32], w[..., 32:]
    zero = jnp.zeros(w.shape[:-1] + (64,), w.dtype)
    return jnp.concatenate([x1, x2, zero], axis=-1), jnp.concatenate([x2, x1, zero], axis=-1)


def _row(v):
    return v.astype(F32).reshape(1, -1)


def _qk_gain_rows(g):
    g = g.astype(F32)
    return jnp.stack([g[:MLA_NOPE], jnp.concatenate([g[MLA_NOPE:], jnp.zeros((64,), F32)])])


def _cross_attention_mlp(x, mem, xa_norm, xa_mem_norm, xa_w_q, xa_w_kv, xa_q_norm, xa_k_norm, xa_w_o,
                         mlp_norm, w_up, w_down):
    k, v = _mem_kv(mem, _row(xa_mem_norm), xa_w_kv.astype(BF16), _row(xa_k_norm))
    x = _xa(x, _row(xa_norm), xa_w_q.astype(BF16), _row(xa_q_norm), k, v, xa_w_o.astype(BF16))
    return _mlp(x, _row(mlp_norm), w_up.astype(BF16), w_down.astype(BF16))


def kernel(x, mem, l0_mix_norm, l0_w_in, l0_q_lora_norm, l0_w_uq, l0_kv_lora_norm, l0_w_ukv, l0_mla_q_norm, l0_mla_k_norm, l0_s5_a_re, l0_s5_a_im, l0_s5_log_dt, l0_s5_b_re, l0_s5_b_im, l0_s5_c_re, l0_s5_c_im, l0_s5_d, l0_s5_glu_val, l0_s5_glu_gate, l0_w_out, l0_xa_norm, l0_xa_mem_norm, l0_xa_w_q, l0_xa_w_kv, l0_xa_q_norm, l0_xa_k_norm, l0_xa_w_o, l0_mlp_norm, l0_w_up, l0_w_down, l1_mix_norm, l1_w_qkv, l1_q_norm, l1_k_norm, l1_w_o, l1_xa_norm, l1_xa_mem_norm, l1_xa_w_q, l1_xa_w_kv, l1_xa_q_norm, l1_xa_k_norm, l1_xa_w_o, l1_mlp_norm, l1_w_up, l1_w_down):
    batch, s, _ = x.shape
    assert batch == 1 and s % MOBA_BLOCK == 0
    x = x.reshape(s, D_MODEL)
    mem = mem.reshape(mem.shape[1], D_MODEL)
    tile = min(FLASH_TILE, s // 2)
    no_slopes = jnp.zeros((MLA_HEADS,), F32)

    w_cq = l0_w_in[:, :Q_LORA]
    w_ckv = l0_w_in[:, Q_LORA:Q_LORA + KV_LORA]
    w_kra, w_krb = _rope_cols(l0_w_in[:, Q_LORA + KV_LORA:Q_LORA + KV_LORA + MLA_ROPE])
    w_u = l0_w_in[:, Q_LORA + KV_LORA + MLA_ROPE:]
    w_in = jnp.concatenate([w_cq, w_ckv, w_kra, w_krb, w_u], axis=1).astype(BF16)
    cq, ckv, kra, krb, u = _l0_in(x, _row(l0_mix_norm), w_in)

    wuq = l0_w_uq.reshape(Q_LORA, MLA_HEADS, MLA_QK)
    wuq_a, wuq_b = _rope_cols(wuq[:, :, MLA_NOPE:])
    wuq = jnp.concatenate([wuq[:, :, :MLA_NOPE], wuq_a, wuq_b], axis=-1).reshape(Q_LORA, MLA_HEADS * 384)
    cos_t, sin_t = _rope_tables(s)
    q, k, v = _mla_qkv(cq, ckv, kra, krb, cos_t, sin_t, _row(l0_q_lora_norm), _row(l0_kv_lora_norm),
                       wuq.astype(BF16), l0_w_ukv.astype(BF16),
                       _qk_gain_rows(l0_mla_q_norm), _qk_gain_rows(l0_mla_k_norm))
    kb, vt = _blocked_kv(k, v, tile)
    y_a = _flash(q, kb, vt, no_slopes, tile=tile, name="flash_mla")

    ops = _s5_operators(l0_s5_a_re, l0_s5_a_im, l0_s5_log_dt, l0_s5_b_re, l0_s5_b_im,
                        l0_s5_c_re, l0_s5_c_im, l0_s5_d, l0_s5_glu_val, l0_s5_glu_gate)
    y_b = _s5(u, *ops)

    w_out = l0_w_out.astype(BF16)
    x = _proj_residual(x, [y_a, y_b], [w_out[:MLA_HEADS * MLA_V], w_out[MLA_HEADS * MLA_V:]])
    x = _cross_attention_mlp(x, mem, l0_xa_norm, l0_xa_mem_norm, l0_xa_w_q, l0_xa_w_kv, l0_xa_q_norm,
                             l0_xa_k_norm, l0_xa_w_o, l0_mlp_norm, l0_w_up, l0_w_down)

    width = MOBA_HEADS * MOBA_HD
    head_gains = jnp.concatenate([jnp.tile(l1_q_norm.astype(F32) * (MOBA_HD ** -0.5 * LOG2E), MOBA_HEADS),
                                  jnp.tile(l1_k_norm.astype(F32), MOBA_HEADS),
                                  jnp.ones((width,), F32)]).reshape(1, 3 * width)
    qkv = _moba_qkv(x, _row(l1_mix_norm), l1_w_qkv.astype(BF16), head_gains)
    kmean = _kmean(qkv)
    n_blk = s // MOBA_BLOCK
    kmean_t = kmean.reshape(n_blk, MOBA_HEADS, MOBA_HD).transpose(1, 0, 2)
    kmean_t = jnp.pad(kmean_t, ((0, 0), (0, LANES - n_blk), (0, 0)))
    slopes = LOG2E * 2.0 ** (-8.0 * jnp.arange(1, MOBA_HEADS + 1, dtype=F32) / MOBA_HEADS)
    q_aug, k_aug = _moba_route(qkv, kmean_t, slopes, tile)
    v1 = qkv[:, 2 * width:].reshape(s, MOBA_HEADS, MOBA_HD).transpose(1, 0, 2)
    kb1, vt1 = _blocked_kv(k_aug, v1, tile)
    o1 = _flash(q_aug, kb1, vt1, slopes, tile=tile, name="flash_moba")
    x = _proj_residual(x, [o1], [l1_w_o.astype(BF16)])
    x = _cross_attention_mlp(x, mem, l1_xa_norm, l1_xa_mem_norm, l1_xa_w_q, l1_xa_w_kv, l1_xa_q_norm,
                             l1_xa_k_norm, l1_xa_w_o, l1_mlp_norm, l1_w_up, l1_w_down)
    return x.reshape(1, s, D_MODEL)
```

```python
import functools
import math

import jax
import jax.numpy as jnp
from jax import lax
from jax.experimental import pallas as pl
from jax.experimental.pallas import tpu as pltpu

F32 = jnp.float32
BF16 = jnp.bfloat16

D_MODEL = 2048
EPS = 1e-6
MASK_VALUE = -1e30

MLA_HEADS = 8
MLA_NOPE = 128
MLA_ROPE = 64
MLA_QK = MLA_NOPE + MLA_ROPE
MLA_V = 128
Q_LORA = 512
KV_LORA = 512
ROPE_BASE = 10000.0
S5_WIDTH = 1024
S5_GROUP = 16
S5_GROUPS = 64
S5_STATE = 64
S5_CHUNK = 16
S5_OCTET = 8
S5_CHUNK_BLOCK = 256
MOBA_HEADS = 16
MOBA_HD = 128
MOBA_BLOCK = 256
MOBA_TOPK = 3
XA_HEADS = 4
XA_HD = 128
D_FF = 4 * D_MODEL

LANES = 128
LOG2E = math.log2(math.e)
FLASH_TILE = 1024
FLASH_STRIP = 256
HEAD_PAD = 256
VT_ROWS = 136
VMEM_LIMIT_BYTES = 56 * 1024 * 1024


def _cparams(*sem):
    return pltpu.CompilerParams(dimension_semantics=sem, vmem_limit_bytes=VMEM_LIMIT_BYTES)


def _rms(x, g):
    return x * lax.rsqrt(jnp.mean(x * x, axis=-1, keepdims=True) + EPS) * g


def _dot(a, b):
    return jnp.dot(a, b, preferred_element_type=F32)


def _dot_nt(a, b):
    return lax.dot_general(a, b, (((1,), (1,)), ((), ())), preferred_element_type=F32)


def _l0_in_kernel(x_ref, g_ref, w_ref, cq_ref, ckv_ref, kra_ref, krb_ref, u_ref):
    xn = _rms(x_ref[...], g_ref[...]).astype(BF16)
    p = _dot(xn, w_ref[...])
    cq_ref[...] = p[:, 0:512]
    ckv_ref[...] = p[:, 512:1024]
    kra_ref[...] = p[:, 1024:1152]
    krb_ref[...] = p[:, 1152:1280]
    u_ref[...] = p[:, 1280:2304]


def _l0_in(x, g, w):
    s = x.shape[0]
    tm = min(512, s)
    n = w.shape[1]
    row = lambda width: pl.BlockSpec((tm, width), lambda i: (i, 0))
    return pl.pallas_call(
        _l0_in_kernel,
        grid=(s // tm,),
        in_specs=[row(D_MODEL), pl.BlockSpec((1, D_MODEL), lambda i: (0, 0)),
                  pl.BlockSpec((D_MODEL, n), lambda i: (0, 0))],
        out_specs=[row(512), row(512), row(128), row(128), row(1024)],
        out_shape=[jax.ShapeDtypeStruct((s, 512), F32), jax.ShapeDtypeStruct((s, 512), F32),
                   jax.ShapeDtypeStruct((s, 128), F32), jax.ShapeDtypeStruct((s, 128), F32),
                   jax.ShapeDtypeStruct((s, 1024), F32)],
        compiler_params=_cparams("parallel"),
        name="l0_in_proj",
    )(x, g, w)


def _mla_qkv_kernel(cq_ref, ckv_ref, kra_ref, krb_ref, cos_ref, sin_ref, gq_ref, gkv_ref,
                    wuq_ref, wukv_ref, qn_ref, kn_ref, q_ref, k_ref, v_ref):
    cos = cos_ref[...]
    sin = sin_ref[...]
    qn = qn_ref[...]
    kn = kn_ref[...]
    qp = _dot(_rms(cq_ref[...], gq_ref[...]).astype(BF16), wuq_ref[...])
    kvp = _dot(_rms(ckv_ref[...], gkv_ref[...]).astype(BF16), wukv_ref[...])
    k_rot = kra_ref[...] * cos + krb_ref[...] * sin
    k_rot_ss = jnp.sum(k_rot * k_rot, axis=-1, keepdims=True)
    q_scale = MLA_QK ** -0.5 * LOG2E
    for h in range(MLA_HEADS):
        qb = h * 384
        nope = qp[:, qb:qb + 128]
        rot = qp[:, qb + 128:qb + 256] * cos + qp[:, qb + 256:qb + 384] * sin
        ss = jnp.sum(nope * nope, axis=-1, keepdims=True) + jnp.sum(rot * rot, axis=-1, keepdims=True)
        inv = lax.rsqrt(ss * (1.0 / MLA_QK) + EPS) * q_scale
        q_ref[h, :, 0:128] = (nope * inv * qn[0:1, :]).astype(BF16)
        q_ref[h, :, 128:256] = (rot * inv * qn[1:2, :]).astype(BF16)
        kb = h * 256
        k_nope = kvp[:, kb:kb + 128]
        ssk = jnp.sum(k_nope * k_nope, axis=-1, keepdims=True) + k_rot_ss
        invk = lax.rsqrt(ssk * (1.0 / MLA_QK) + EPS)
        k_ref[h, :, 0:128] = (k_nope * invk * kn[0:1, :]).astype(BF16)
        k_ref[h, :, 128:256] = (k_rot * invk * kn[1:2, :]).astype(BF16)
        v_ref[h] = kvp[:, kb + 128:kb + 256].astype(BF16)


def _mla_qkv(cq, ckv, kra, krb, cos_t, sin_t, gq, gkv, wuq, wukv, qn, kn):
    s = cq.shape[0]
    tm = min(256, s)
    row = lambda width: pl.BlockSpec((tm, width), lambda i: (i, 0))
    full = lambda a: pl.BlockSpec(a.shape, lambda i: (0,) * a.ndim)
    head_out = lambda width: pl.BlockSpec((MLA_HEADS, tm, width), lambda i: (0, i, 0))
    return pl.pallas_call(
        _mla_qkv_kernel,
        grid=(s // tm,),
        in_specs=[row(512), row(512), row(128), row(128), row(128), row(128),
                  full(gq), full(gkv), full(wuq), full(wukv), full(qn), full(kn)],
        out_specs=[head_out(HEAD_PAD), head_out(HEAD_PAD), head_out(MLA_V)],
        out_shape=[jax.ShapeDtypeStruct((MLA_HEADS, s, HEAD_PAD), BF16),
                   jax.ShapeDtypeStruct((MLA_HEADS, s, HEAD_PAD), BF16),
                   jax.ShapeDtypeStruct((MLA_HEADS, s, MLA_V), BF16)],
        compiler_params=_cparams("parallel"),
        name="mla_qkv",
    )(cq, ckv, kra, krb, cos_t, sin_t, gq, gkv, wuq, wukv, qn, kn)


def _flash_kernel(slope_ref, q_ref, k_ref, vt_ref, o_ref, m_ref, acc_ref, s_ref, cm_ref, *, tile, strip):
    h = pl.program_id(0)
    pair = pl.program_id(1)
    base = 2 * pair
    slope_tile = slope_ref[h] * tile
    m_ref[...] = jnp.full(m_ref.shape, -jnp.inf, F32)
    acc_ref[...] = jnp.zeros(acc_ref.shape, F32)
    n_strips = tile // strip
    strips = [(c, s) for c in range(2) for s in range(n_strips)]

    def cols(s):
        return slice(s * strip, (s + 1) * strip)

    def scores(c, s, kj, diagonal):
        q = q_ref[c * tile + s * strip:c * tile + (s + 1) * strip, :]
        st = _dot_nt(k_ref[kj], q)
        if diagonal:
            key_idx = lax.broadcasted_iota(jnp.int32, (tile, strip), 0)
            qry_idx = lax.broadcasted_iota(jnp.int32, (tile, strip), 1) + s * strip
            st = jnp.where(key_idx <= qry_idx, st, MASK_VALUE)
        return st, jnp.max(st, axis=0, keepdims=True)

    def softmax_pv(c, s, kj, st, col_max):
        shift = slope_tile * (kj - (base + c)).astype(F32)
        m_prev = m_ref[c, :, cols(s)]
        m_new = jnp.maximum(m_prev, col_max + shift)
        alpha = jnp.exp2(m_prev - m_new)
        p = jnp.exp2(st - (m_new - shift)).astype(BF16)
        acc_ref[c, :, cols(s)] = alpha * acc_ref[c, :, cols(s)] + _dot(vt_ref[kj], p)
        m_ref[c, :, cols(s)] = m_new

    def hold(c, s, st, col_max):
        s_ref[c, :, cols(s)] = st
        cm_ref[c, :, cols(s)] = col_max

    def held(c, s):
        return s_ref[c, :, cols(s)], cm_ref[c, :, cols(s)]

    pending = None
    for s in range(n_strips):
        current = scores(1, s, base, False)
        if pending is not None:
            softmax_pv(1, s - 1, base, *pending)
        pending = current
    for i, (c, s) in enumerate(strips):
        diag = scores(c, s, base + c, True)
        if i == 0:
            softmax_pv(1, n_strips - 1, base, *pending)
        hold(c, s, *diag)

    def body(j, carry):
        for c, s in strips:
            nxt = scores(c, s, j, False)
            softmax_pv(c, s, jnp.where(j == 0, base + c, j - 1), *held(c, s))
            hold(c, s, *nxt)
        return carry

    lax.fori_loop(0, base, body, 0)
    for c, s in strips:
        softmax_pv(c, s, jnp.where(base == 0, base + c, base - 1), *held(c, s))
    for c in range(2):
        acc = acc_ref[c]
        out_t = acc[0:128, :] / acc[128:129, :]
        o_ref[c * tile:(c + 1) * tile, :] = out_t.T.astype(o_ref.dtype)


def _flash(q, k, vt, slopes, *, tile, name):
    heads, s, _ = q.shape
    nk = s // tile
    return pl.pallas_call(
        functools.partial(_flash_kernel, tile=tile, strip=min(FLASH_STRIP, tile)),
        grid_spec=pltpu.PrefetchScalarGridSpec(
            num_scalar_prefetch=1,
            grid=(heads, s // (2 * tile)),
            in_specs=[pl.BlockSpec((None, 2 * tile, HEAD_PAD), lambda h, i, sl: (h, i, 0)),
                      pl.BlockSpec((None, nk, tile, HEAD_PAD), lambda h, i, sl: (h, 0, 0, 0)),
                      pl.BlockSpec((None, nk, VT_ROWS, tile), lambda h, i, sl: (h, 0, 0, 0))],
            out_specs=pl.BlockSpec((2 * tile, 128), lambda h, i, sl: (i, h)),
            scratch_shapes=[pltpu.VMEM((2, 1, tile), F32), pltpu.VMEM((2, VT_ROWS, tile), F32),
                            pltpu.VMEM((2, tile, tile), F32), pltpu.VMEM((2, 1, tile), F32)],
        ),
        out_shape=jax.ShapeDtypeStruct((s, heads * 128), BF16),
        compiler_params=_cparams("parallel", "arbitrary"),
        name=name,
    )(slopes, q, k, vt)


def _blocked_kv(k, v, tile):
    heads, s, _ = k.shape
    nk = s // tile
    kb = k.reshape(heads, nk, tile, HEAD_PAD)
    vt = jnp.swapaxes(v.reshape(heads, nk, tile, 128), 2, 3)
    ones = jnp.ones((heads, nk, 1, tile), v.dtype)
    zeros = jnp.zeros((heads, nk, VT_ROWS - 129, tile), v.dtype)
    return kb, jnp.concatenate([vt, ones, zeros], axis=2)


def _s5_kernel(u_ref, lag_ref, wx_ref, vo_ref, a_ref, d_ref, glu_ref, o_ref, carry_ref, x_ref, s_ref, m_ref):
    cb = pl.program_id(1)
    n_chunks = u_ref.shape[0]
    half = S5_OCTET * S5_STATE

    @pl.when(cb == 0)
    def _():
        carry_ref[...] = jnp.zeros(carry_ref.shape, F32)
        for j in range(S5_CHUNK):
            for t in range(j, S5_CHUNK):
                m_ref[j * LANES:(j + 1) * LANES, t * LANES:(t + 1) * LANES] = lag_ref[t - j]
        for t in range(0, S5_CHUNK, 2):
            m_ref[(t + 1) * LANES:(t + 2) * LANES, t * LANES:(t + 1) * LANES] = jnp.zeros((LANES, LANES), BF16)

    steps = [u_ref[:, j, :] for j in range(S5_CHUNK)]
    xb = jnp.concatenate([p.astype(BF16) for p in steps], axis=1)
    x_ref[...] = _dot(xb, wx_ref[...])
    a_re = a_ref[0:1, :]
    a_im = a_ref[1:2, :]

    def body(c, carry):
        re, im = carry
        s_ref[pl.ds(c, 1), :] = jnp.concatenate([re, im], axis=1)
        x = x_ref[pl.ds(c, 1), :]
        return (a_re * re - a_im * im + x[:, :half], a_re * im + a_im * re + x[:, half:])

    re, im = lax.fori_loop(0, n_chunks, body, (carry_ref[0:1, :], carry_ref[1:2, :]), unroll=8)
    carry_ref[0:1, :] = re
    carry_ref[1:2, :] = im

    sb = s_ref[...].astype(BF16)
    d = d_ref[...]
    for tp in range(S5_CHUNK // 2):
        lo, hi = 2 * tp * LANES, (2 * tp + 2) * LANES
        y2 = _dot(xb[:, :hi], m_ref[0:hi, lo:hi]) + _dot(sb, vo_ref[:, lo:hi])
        ys = [jax.nn.gelu(y2[:, k * LANES:(k + 1) * LANES] + d * steps[2 * tp + k]).astype(BF16)
              for k in range(2)]
        vg = _dot(jnp.concatenate(ys, axis=0), glu_ref[...])
        out = vg[:, :LANES] * jax.nn.sigmoid(vg[:, LANES:])
        for k in range(2):
            o_ref[:, 2 * tp + k, :] = out[k * n_chunks:(k + 1) * n_chunks, :]


def _s5(u, lags, wx, vo, a, d, glu):
    s = u.shape[0]
    n_chunks = s // S5_CHUNK
    cb = min(S5_CHUNK_BLOCK, n_chunks)
    octets = S5_GROUPS // S5_OCTET
    u3 = u.reshape(n_chunks, S5_CHUNK, S5_WIDTH)
    per_octet = lambda arr: pl.BlockSpec((None,) + arr.shape[1:], lambda o, i: (o,) + (0,) * (arr.ndim - 1),
                                         pipeline_mode=pl.Buffered(1))
    io_spec = pl.BlockSpec((cb, S5_CHUNK, LANES), lambda o, i: (i, 0, o))
    state_w = 2 * S5_OCTET * S5_STATE
    y3 = pl.pallas_call(
        _s5_kernel,
        grid=(octets, n_chunks // cb),
        in_specs=[io_spec] + [per_octet(arr) for arr in (lags, wx, vo, a, d, glu)],
        out_specs=io_spec,
        out_shape=jax.ShapeDtypeStruct((n_chunks, S5_CHUNK, S5_WIDTH), F32),
        scratch_shapes=[pltpu.VMEM((2, state_w // 2), F32), pltpu.VMEM((cb, state_w), F32),
                        pltpu.VMEM((cb, state_w), F32),
                        pltpu.VMEM((S5_CHUNK * LANES, S5_CHUNK * LANES), BF16)],
        compiler_params=_cparams("parallel", "arbitrary"),
        name="s5_scan",
    )(u3, lags, wx, vo, a, d, glu)
    return y3.reshape(s, S5_WIDTH)


def _s5_operators(a_re, a_im, log_dt, b_re, b_im, c_re, c_im, d_skip, glu_val, glu_gate):
    hp = lax.Precision.HIGHEST
    a_re, a_im = a_re.astype(F32), a_im.astype(F32)
    dt = jnp.exp(log_dt.astype(F32))[:, None]
    er, ei = a_re * dt, a_im * dt

    def lam_pow(k):
        kk = k.astype(F32)[:, None, None]
        mag = jnp.exp(er[None] * kk)
        return mag * jnp.cos(ei[None] * kk), mag * jnp.sin(ei[None] * kk)

    l1r, l1i = lam_pow(jnp.arange(1, 2))
    nr, ni = l1r[0] - 1.0, l1i[0]
    den = a_re * a_re + a_im * a_im
    fr, fi = (nr * a_re + ni * a_im) / den, (ni * a_re - nr * a_im) / den
    b_re, b_im = b_re.astype(F32), b_im.astype(F32)
    bbr = fr[:, :, None] * b_re - fi[:, :, None] * b_im
    bbi = fr[:, :, None] * b_im + fi[:, :, None] * b_re
    c_re, c_im = c_re.astype(F32), c_im.astype(F32)

    L, A, N, P = S5_CHUNK, S5_OCTET, S5_STATE, S5_GROUP
    O = S5_GROUPS // A
    same = jnp.eye(A, dtype=bool)
    pr, pi = lam_pow(jnp.arange(0, L + 1))
    lbr = pr[:L, :, :, None] * bbr[None] - pi[:L, :, :, None] * bbi[None]
    lbi = pr[:L, :, :, None] * bbi[None] + pi[:L, :, :, None] * bbr[None]
    kern = (jnp.einsum('gpn,lgnq->lgqp', c_re, lbr, precision=hp)
            - jnp.einsum('gpn,lgnq->lgqp', c_im, lbi, precision=hp))
    lags = kern.reshape(L, O, A, P, P).transpose(1, 0, 2, 3, 4)
    lags = jnp.where(same[None, None, :, None, :, None], lags[:, :, :, :, None, :], 0.0)
    lags = lags.astype(BF16).reshape(O, L, A * P, A * P)
    rev = L - 1 - jnp.arange(L)

    def state_cols(w):
        w = w.reshape(L, O, A, N, P).transpose(1, 0, 2, 4, 3)
        w = jnp.where(same[None, None, :, None, :, None], w[:, :, :, :, None, :], 0.0)
        return w.astype(BF16).reshape(O, L * A * P, A * N)

    wx = jnp.concatenate([state_cols(lbr[rev]), state_cols(lbi[rev])], axis=-1)
    p1r, p1i = pr[1:], pi[1:]
    cr = c_re.transpose(0, 2, 1)[None]
    ci = c_im.transpose(0, 2, 1)[None]
    vr = cr * p1r[..., None] - ci * p1i[..., None]
    vi = cr * p1i[..., None] + ci * p1r[..., None]

    def state_rows(v):
        v = v.reshape(L, O, A, N, P).transpose(1, 2, 3, 0, 4)
        v = jnp.where(same[None, :, None, None, :, None], v[:, :, :, :, None, :], 0.0)
        return v.astype(BF16).reshape(O, A * N, L * A * P)

    vo = jnp.concatenate([state_rows(vr), state_rows(-vi)], axis=1)
    a = jnp.stack([pr[L].reshape(O, A * N), pi[L].reshape(O, A * N)], axis=1)
    d = d_skip.astype(F32).reshape(O, 1, A * P)

    def glu_cols(w):
        w = w.astype(F32).reshape(O, A, P, P)
        w = jnp.where(same[None, :, None, :, None], w[:, :, :, None, :], 0.0)
        return w.astype(BF16).reshape(O, A * P, A * P)

    glu = jnp.concatenate([glu_cols(glu_val), glu_cols(glu_gate)], axis=-1)
    return lags, wx, vo, a, d, glu


def _proj_residual_kernel(*refs, n_in):
    x_ref = refs[0]
    o_ref = refs[1 + 2 * n_in]
    acc = x_ref[...]
    for i in range(n_in):
        acc = acc + _dot(refs[1 + i][...].astype(BF16), refs[1 + n_in + i][...])
    o_ref[...] = acc


def _proj_residual(x, acts, weights):
    s = x.shape[0]
    tm = min(512, s)
    n_in = len(acts)
    return pl.pallas_call(
        functools.partial(_proj_residual_kernel, n_in=n_in),
        grid=(s // tm,),
        in_specs=([pl.BlockSpec((tm, D_MODEL), lambda i: (i, 0))]
                  + [pl.BlockSpec((tm, a.shape[1]), lambda i: (i, 0)) for a in acts]
                  + [pl.BlockSpec(w.shape, lambda i: (0, 0)) for w in weights]),
        out_specs=pl.BlockSpec((tm, D_MODEL), lambda i: (i, 0)),
        out_shape=jax.ShapeDtypeStruct((s, D_MODEL), F32),
        compiler_params=_cparams("parallel"),
        name="proj_residual",
    )(x, *acts, *weights)


def _mem_kv_kernel(mem_ref, g_ref, w_ref, kn_ref, k_ref, v_ref):
    kv = _dot(_rms(mem_ref[...], g_ref[...]).astype(BF16), w_ref[...])
    width = XA_HEADS * XA_HD
    for h in range(XA_HEADS):
        k_ref[h] = _rms(kv[:, h * XA_HD:(h + 1) * XA_HD], kn_ref[...]).astype(BF16)
        v_ref[h] = kv[:, width + h * XA_HD:width + (h + 1) * XA_HD].astype(BF16)


def _mem_kv(mem, g, w, kn):
    m_len = mem.shape[0]
    shape = jax.ShapeDtypeStruct((XA_HEADS, m_len, XA_HD), BF16)
    return pl.pallas_call(
        _mem_kv_kernel,
        out_shape=[shape, shape],
        compiler_params=pltpu.CompilerParams(vmem_limit_bytes=VMEM_LIMIT_BYTES),
        name="xa_mem_kv",
    )(mem, g, w, kn)


def _xa_kernel(x_ref, g_ref, wq_ref, qn_ref, k_ref, v_ref, wo_ref, o_ref):
    x = x_ref[...]
    q = _dot(_rms(x, g_ref[...]).astype(BF16), wq_ref[...])
    qn = qn_ref[...] * (XA_HD ** -0.5)
    heads = []
    for h in range(XA_HEADS):
        qh = _rms(q[:, h * XA_HD:(h + 1) * XA_HD], qn).astype(BF16)
        sc = _dot_nt(qh, k_ref[h])
        sc = sc - jnp.max(sc, axis=-1, keepdims=True)
        p = jnp.exp(sc)
        p = p / jnp.sum(p, axis=-1, keepdims=True)
        heads.append(_dot(p.astype(BF16), v_ref[h]).astype(BF16))
    o = jnp.concatenate(heads, axis=-1)
    o_ref[...] = x + _dot(o, wo_ref[...])


def _xa(x, g, wq, qn, k, v, wo):
    s = x.shape[0]
    tm = min(512, s)
    full = lambda a: pl.BlockSpec(a.shape, lambda i: (0,) * a.ndim)
    return pl.pallas_call(
        _xa_kernel,
        grid=(s // tm,),
        in_specs=[pl.BlockSpec((tm, D_MODEL), lambda i: (i, 0)), full(g), full(wq), full(qn),
                  full(k), full(v), full(wo)],
        out_specs=pl.BlockSpec((tm, D_MODEL), lambda i: (i, 0)),
        out_shape=jax.ShapeDtypeStruct((s, D_MODEL), F32),
        compiler_params=_cparams("parallel"),
        name="xa",
    )(x, g, wq, qn, k, v, wo)


def _mlp_kernel(x_ref, g_ref, wu_ref, wd_ref, o_ref, xn_ref, *, sub):
    j = pl.program_id(1)

    @pl.when(j == 0)
    def _():
        x = x_ref[...]
        xn_ref[...] = _rms(x, g_ref[...]).astype(BF16)
        o_ref[...] = x

    xn = xn_ref[...]
    n_sub = wu_ref.shape[1] // sub
    hids = [_dot(xn, wu_ref[:, i * sub:(i + 1) * sub]) for i in range(n_sub)]
    acc = o_ref[...]
    for i in range(n_sub):
        hid = jnp.maximum(hids[i], 0.0)
        acc = acc + _dot((hid * hid).astype(BF16), wd_ref[i * sub:(i + 1) * sub, :])
    o_ref[...] = acc


def _mlp(x, g, wu, wd):
    s = x.shape[0]
    tm = min(512, s)
    tf = 2048
    return pl.pallas_call(
        functools.partial(_mlp_kernel, sub=512),
        grid=(s // tm, D_FF // tf),
        in_specs=[pl.BlockSpec((tm, D_MODEL), lambda i, j: (i, 0)),
                  pl.BlockSpec((1, D_MODEL), lambda i, j: (0, 0)),
                  pl.BlockSpec((D_MODEL, tf), lambda i, j: (0, j)),
                  pl.BlockSpec((tf, D_MODEL), lambda i, j: (j, 0))],
        out_specs=pl.BlockSpec((tm, D_MODEL), lambda i, j: (i, 0)),
        out_shape=jax.ShapeDtypeStruct((s, D_MODEL), F32),
        scratch_shapes=[pltpu.VMEM((tm, D_MODEL), BF16)],
        compiler_params=_cparams("parallel", "arbitrary"),
        name="mlp",
    )(x, g, wu, wd)


def _moba_qkv_kernel(x_ref, g_ref, w_ref, hn_ref, o_ref, xn_ref, *, heads_per_block, norm_blocks, row_split):
    j = pl.program_id(1)

    @pl.when(j == 0)
    def _():
        xn_ref[...] = _rms(x_ref[...], g_ref[...]).astype(BF16)

    normed = j < norm_blocks
    rows = xn_ref.shape[0] // row_split
    products = [_dot(xn_ref[r * rows:(r + 1) * rows, :], w_ref[...]) for r in range(row_split)]
    for r, p in enumerate(products):
        for h in range(heads_per_block):
            sl = slice(h * MOBA_HD, (h + 1) * MOBA_HD)
            ph = p[:, sl]
            inv = lax.rsqrt(jnp.mean(ph * ph, axis=-1, keepdims=True) + EPS)
            o_ref[r * rows:(r + 1) * rows, sl] = (ph * jnp.where(normed, inv, 1.0) * hn_ref[:, sl]).astype(o_ref.dtype)


def _moba_qkv(x, g, w, head_gains):
    s = x.shape[0]
    tm = min(1024, s)
    tn = 512
    n = w.shape[1]
    kernel = functools.partial(_moba_qkv_kernel, heads_per_block=tn // MOBA_HD,
                               norm_blocks=2 * MOBA_HEADS * MOBA_HD // tn, row_split=4 if tm % 1024 == 0 else 1)
    return pl.pallas_call(
        kernel,
        grid=(s // tm, n // tn),
        in_specs=[pl.BlockSpec((tm, D_MODEL), lambda i, j: (i, 0)),
                  pl.BlockSpec((1, D_MODEL), lambda i, j: (0, 0)),
                  pl.BlockSpec((D_MODEL, tn), lambda i, j: (0, j)),
                  pl.BlockSpec((1, tn), lambda i, j: (0, j))],
        out_specs=pl.BlockSpec((tm, tn), lambda i, j: (i, j)),
        out_shape=jax.ShapeDtypeStruct((s, n), BF16),
        scratch_shapes=[pltpu.VMEM((tm, D_MODEL), BF16)],
        compiler_params=_cparams("parallel", "arbitrary"),
        name="moba_qkv",
    )(x, g, w, head_gains)


def _kmean_kernel(k_ref, o_ref, *, blocks):
    for b in range(blocks):
        kb = k_ref[b * MOBA_BLOCK:(b + 1) * MOBA_BLOCK, :].astype(F32)
        o_ref[b:b + 1, :] = jnp.mean(kb, axis=0, keepdims=True)


def _kmean(qkv):
    s = qkv.shape[0]
    width = MOBA_HEADS * MOBA_HD
    n_blk = s // MOBA_BLOCK
    blocks = 8 if n_blk % 8 == 0 else n_blk
    rows = blocks * MOBA_BLOCK
    return pl.pallas_call(
        functools.partial(_kmean_kernel, blocks=blocks),
        grid=(s // rows, width // 512),
        in_specs=[pl.BlockSpec((rows, 512), lambda i, j: (i, width // 512 + j))],
        out_specs=pl.BlockSpec((blocks, 512), lambda i, j: (i, j)),
        out_shape=jax.ShapeDtypeStruct((n_blk, width), F32),
        compiler_params=_cparams("parallel", "parallel"),
        name="moba_kmean",
    )(qkv)


def _split3(x):
    hi = x.astype(BF16)
    r1 = x - hi.astype(F32)
    mid = r1.astype(BF16)
    lo = (r1 - mid.astype(F32)).astype(BF16)
    return hi.astype(F32), mid.astype(F32), lo.astype(F32)


ROUTE_LANES = 64
SLOPE_RC_LANE = 64
SLOPE_RB_LANE = 67
QUERY_BIAS_LANE = 70


def _moba_route_kernel(q_ref, k_ref, km_ref, slope_ref, slope_cols_ref, qa_ref, ka_ref, *, blocks_per_tile):
    i = pl.program_id(0)
    shape = (MOBA_BLOCK, LANES)
    lane = lax.broadcasted_iota(jnp.int32, shape, 1)
    lane_f = lane.astype(F32)
    row_f = lax.broadcasted_iota(jnp.int32, shape, 0).astype(F32)
    blk_in_tile = (i % blocks_per_tile).astype(F32)
    offset_in_tile = blk_in_tile * MOBA_BLOCK + row_f
    k_extra = jnp.where(lane == i, 1.0, 0.0)
    k_extra = jnp.where((lane >= SLOPE_RC_LANE) & (lane < SLOPE_RC_LANE + 3), row_f, k_extra)
    k_extra = jnp.where((lane >= SLOPE_RB_LANE) & (lane < SLOPE_RB_LANE + 3), blk_in_tile, k_extra)
    k_extra = jnp.where((lane >= QUERY_BIAS_LANE) & (lane < QUERY_BIAS_LANE + 3), 1.0, k_extra)
    k_extra = k_extra.astype(BF16)
    for h in range(MOBA_HEADS):
        sl = slice(h * MOBA_HD, (h + 1) * MOBA_HD)
        q = q_ref[:, sl]
        gate = lax.dot_general(q.astype(F32), km_ref[h], (((1,), (1,)), ((), ())),
                               precision=lax.Precision.HIGHEST, preferred_element_type=F32)
        gate = jnp.where(lane < i, gate, -jnp.inf)
        chosen = lane == i
        for r in range(MOBA_TOPK):
            best = jnp.max(gate, axis=-1, keepdims=True)
            idx = jnp.min(jnp.where(gate == best, lane_f, float(LANES)), axis=-1, keepdims=True)
            pick = lane_f == idx
            chosen = jnp.logical_or(chosen, jnp.logical_and(pick, r < i))
            gate = jnp.where(pick, -jnp.inf, gate)
        q_extra = jnp.where(chosen, 0.0, MASK_VALUE)
        q_extra = jnp.where(lane < ROUTE_LANES, q_extra, slope_cols_ref[h])
        bias = _split3(-slope_ref[h] * offset_in_tile)
        for piece in range(3):
            q_extra = jnp.where(lane == QUERY_BIAS_LANE + piece, bias[piece], q_extra)
        qa_ref[h, :, 0:128] = q
        qa_ref[h, :, 128:256] = q_extra.astype(BF16)
        ka_ref[h, :, 0:128] = k_ref[:, sl]
        ka_ref[h, :, 128:256] = k_extra


def _moba_route(qkv, kmean_t, slopes, tile):
    s = qkv.shape[0]
    n_blk = s // MOBA_BLOCK
    width = MOBA_HEADS * MOBA_HD
    aug = jax.ShapeDtypeStruct((MOBA_HEADS, s, HEAD_PAD), BF16)
    pieces = _split3(slopes)
    lane = jnp.arange(LANES)
    cols = jnp.zeros((MOBA_HEADS, LANES), F32)
    for piece in range(3):
        cols = jnp.where(lane[None, :] == SLOPE_RC_LANE + piece, pieces[piece][:, None], cols)
        cols = jnp.where(lane[None, :] == SLOPE_RB_LANE + piece, MOBA_BLOCK * pieces[piece][:, None], cols)
    slope_rows = jnp.broadcast_to(slopes[:, None, None], (MOBA_HEADS, 1, LANES))
    full = lambda a: pl.BlockSpec(a.shape, lambda i: (0,) * a.ndim)
    return pl.pallas_call(
        functools.partial(_moba_route_kernel, blocks_per_tile=tile // MOBA_BLOCK),
        grid=(n_blk,),
        in_specs=[pl.BlockSpec((MOBA_BLOCK, width), lambda i: (i, 0)),
                  pl.BlockSpec((MOBA_BLOCK, width), lambda i: (i, 1)),
                  full(kmean_t), full(slope_rows), full(cols[:, None, :])],
        out_specs=[pl.BlockSpec((MOBA_HEADS, MOBA_BLOCK, HEAD_PAD), lambda i: (0, i, 0))] * 2,
        out_shape=[aug, aug],
        compiler_params=_cparams("parallel"),
        name="moba_route",
    )(qkv, qkv, kmean_t, slope_rows, cols[:, None, :])


def _rope_tables(s):
    inv_freq = 1.0 / (ROPE_BASE ** (jnp.arange(0, MLA_ROPE, 2, dtype=F32) / MLA_ROPE))
    ang = jnp.arange(s, dtype=F32)[:, None] * inv_freq[None, :]
    cos, sin = jnp.cos(ang), jnp.sin(ang)
    zero = jnp.zeros((s, 64), F32)
    return jnp.concatenate([cos, cos, zero], axis=1), jnp.concatenate([-sin, sin, zero], axis=1)


def _rope_cols(w):
    x1, x2 = w[..., :32], w[..., 32:]
    zero = jnp.zeros(w.shape[:-1] + (64,), w.dtype)
    return jnp.concatenate([x1, x2, zero], axis=-1), jnp.concatenate([x2, x1, zero], axis=-1)


def _row(v):
    return v.astype(F32).reshape(1, -1)


def _qk_gain_rows(g):
    g = g.astype(F32)
    return jnp.stack([g[:MLA_NOPE], jnp.concatenate([g[MLA_NOPE:], jnp.zeros((64,), F32)])])


def _cross_attention_mlp(x, mem, xa_norm, xa_mem_norm, xa_w_q, xa_w_kv, xa_q_norm, xa_k_norm, xa_w_o,
                         mlp_norm, w_up, w_down):
    k, v = _mem_kv(mem, _row(xa_mem_norm), xa_w_kv.astype(BF16), _row(xa_k_norm))
    x = _xa(x, _row(xa_norm), xa_w_q.astype(BF16), _row(xa_q_norm), k, v, xa_w_o.astype(BF16))
    return _mlp(x, _row(mlp_norm), w_up.astype(BF16), w_down.astype(BF16))


def kernel(x, mem, l0_mix_norm, l0_w_in, l0_q_lora_norm, l0_w_uq, l0_kv_lora_norm, l0_w_ukv, l0_mla_q_norm, l0_mla_k_norm, l0_s5_a_re, l0_s5_a_im, l0_s5_log_dt, l0_s5_b_re, l0_s5_b_im, l0_s5_c_re, l0_s5_c_im, l0_s5_d, l0_s5_glu_val, l0_s5_glu_gate, l0_w_out, l0_xa_norm, l0_xa_mem_norm, l0_xa_w_q, l0_xa_w_kv, l0_xa_q_norm, l0_xa_k_norm, l0_xa_w_o, l0_mlp_norm, l0_w_up, l0_w_down, l1_mix_norm, l1_w_qkv, l1_q_norm, l1_k_norm, l1_w_o, l1_xa_norm, l1_xa_mem_norm, l1_xa_w_q, l1_xa_w_kv, l1_xa_q_norm, l1_xa_k_norm, l1_xa_w_o, l1_mlp_norm, l1_w_up, l1_w_down):
    batch, s, _ = x.shape
    assert batch == 1 and s % MOBA_BLOCK == 0
    x = x.reshape(s, D_MODEL)
    mem = mem.reshape(mem.shape[1], D_MODEL)
    tile = min(FLASH_TILE, s // 2)
    no_slopes = jnp.zeros((MLA_HEADS,), F32)

    w_cq = l0_w_in[:, :Q_LORA]
    w_ckv = l0_w_in[:, Q_LORA:Q_LORA + KV_LORA]
    w_kra, w_krb = _rope_cols(l0_w_in[:, Q_LORA + KV_LORA:Q_LORA + KV_LORA + MLA_ROPE])
    w_u = l0_w_in[:, Q_LORA + KV_LORA + MLA_ROPE:]
    w_in = jnp.concatenate([w_cq, w_ckv, w_kra, w_krb, w_u], axis=1).astype(BF16)
    cq, ckv, kra, krb, u = _l0_in(x, _row(l0_mix_norm), w_in)

    wuq = l0_w_uq.reshape(Q_LORA, MLA_HEADS, MLA_QK)
    wuq_a, wuq_b = _rope_cols(wuq[:, :, MLA_NOPE:])
    wuq = jnp.concatenate([wuq[:, :, :MLA_NOPE], wuq_a, wuq_b], axis=-1).reshape(Q_LORA, MLA_HEADS * 384)
    cos_t, sin_t = _rope_tables(s)
    q, k, v = _mla_qkv(cq, ckv, kra, krb, cos_t, sin_t, _row(l0_q_lora_norm), _row(l0_kv_lora_norm),
                       wuq.astype(BF16), l0_w_ukv.astype(BF16),
                       _qk_gain_rows(l0_mla_q_norm), _qk_gain_rows(l0_mla_k_norm))
    kb, vt = _blocked_kv(k, v, tile)
    y_a = _flash(q, kb, vt, no_slopes, tile=tile, name="flash_mla")

    ops = _s5_operators(l0_s5_a_re, l0_s5_a_im, l0_s5_log_dt, l0_s5_b_re, l0_s5_b_im,
                        l0_s5_c_re, l0_s5_c_im, l0_s5_d, l0_s5_glu_val, l0_s5_glu_gate)
    y_b = _s5(u, *ops)

    w_out = l0_w_out.astype(BF16)
    x = _proj_residual(x, [y_a, y_b], [w_out[:MLA_HEADS * MLA_V], w_out[MLA_HEADS * MLA_V:]])
    x = _cross_attention_mlp(x, mem, l0_xa_norm, l0_xa_mem_norm, l0_xa_w_q, l0_xa_w_kv, l0_xa_q_norm,
                             l0_xa_k_norm, l0_xa_w_o, l0_mlp_norm, l0_w_up, l0_w_down)

    width = MOBA_HEADS * MOBA_HD
    head_gains = jnp.concatenate([jnp.tile(l1_q_norm.astype(F32) * (MOBA_HD ** -0.5 * LOG2E), MOBA_HEADS),
                                  jnp.tile(l1_k_norm.astype(F32), MOBA_HEADS),
                                  jnp.ones((width,), F32)]).reshape(1, 3 * width)
    qkv = _moba_qkv(x, _row(l1_mix_norm), l1_w_qkv.astype(BF16), head_gains)
    kmean = _kmean(qkv)
    n_blk = s // MOBA_BLOCK
    kmean_t = kmean.reshape(n_blk, MOBA_HEADS, MOBA_HD).transpose(1, 0, 2)
    kmean_t = jnp.pad(kmean_t, ((0, 0), (0, LANES - n_blk), (0, 0)))
    slopes = LOG2E * 2.0 ** (-8.0 * jnp.arange(1, MOBA_HEADS + 1, dtype=F32) / MOBA_HEADS)
    q_aug, k_aug = _moba_route(qkv, kmean_t, slopes, tile)
    v1 = qkv[:, 2 * width:].reshape(s, MOBA_HEADS, MOBA_HD).transpose(1, 0, 2)
    kb1, vt1 = _blocked_kv(k_aug, v1, tile)
    o1 = _flash(q_aug, kb1, vt1, slopes, tile=tile, name="flash_moba")
    x = _proj_residual(x, [o1], [l1_w_o.astype(BF16)])
    x = _cross_attention_mlp(x, mem, l1_xa_norm, l1_xa_mem_norm, l1_xa_w_q, l1_xa_w_kv, l1_xa_q_norm,
                             l1_xa_k_norm, l1_xa_w_o, l1_mlp_norm, l1_w_up, l1_w_down)
    return x.reshape(1, s, D_MODEL)
```

```python
import functools
import math

import jax
import jax.numpy as jnp
from jax import lax
from jax.experimental import pallas as pl
from jax.experimental.pallas import tpu as pltpu

F32 = jnp.float32
BF16 = jnp.bfloat16

D_MODEL = 2048
EPS = 1e-6
MASK_VALUE = -1e30

MLA_HEADS = 8
MLA_NOPE = 128
MLA_ROPE = 64
MLA_QK = MLA_NOPE + MLA_ROPE
MLA_V = 128
Q_LORA = 512
KV_LORA = 512
ROPE_BASE = 10000.0
S5_WIDTH = 1024
S5_GROUP = 16
S5_GROUPS = 64
S5_STATE = 64
S5_CHUNK = 16
S5_OCTET = 8
S5_CHUNK_BLOCK = 256
MOBA_HEADS = 16
MOBA_HD = 128
MOBA_BLOCK = 256
MOBA_TOPK = 3
XA_HEADS = 4
XA_HD = 128
D_FF = 4 * D_MODEL

LANES = 128
LOG2E = math.log2(math.e)
FLASH_TILE = 1024
FLASH_STRIP = 256
HEAD_PAD = 256
VT_ROWS = 136
VMEM_LIMIT_BYTES = 56 * 1024 * 1024


def _cparams(*sem):
    return pltpu.CompilerParams(dimension_semantics=sem, vmem_limit_bytes=VMEM_LIMIT_BYTES)


def _rms(x, g):
    return x * lax.rsqrt(jnp.mean(x * x, axis=-1, keepdims=True) + EPS) * g


def _dot(a, b):
    return jnp.dot(a, b, preferred_element_type=F32)


def _dot_nt(a, b):
    return lax.dot_general(a, b, (((1,), (1,)), ((), ())), preferred_element_type=F32)


def _l0_in_kernel(x_ref, g_ref, w_ref, cq_ref, ckv_ref, kra_ref, krb_ref, u_ref):
    xn = _rms(x_ref[...], g_ref[...]).astype(BF16)
    p = _dot(xn, w_ref[...])
    cq_ref[...] = p[:, 0:512]
    ckv_ref[...] = p[:, 512:1024]
    kra_ref[...] = p[:, 1024:1152]
    krb_ref[...] = p[:, 1152:1280]
    u_ref[...] = p[:, 1280:2304]


def _l0_in(x, g, w):
    s = x.shape[0]
    tm = min(512, s)
    n = w.shape[1]
    row = lambda width: pl.BlockSpec((tm, width), lambda i: (i, 0))
    return pl.pallas_call(
        _l0_in_kernel,
        grid=(s // tm,),
        in_specs=[row(D_MODEL), pl.BlockSpec((1, D_MODEL), lambda i: (0, 0)),
                  pl.BlockSpec((D_MODEL, n), lambda i: (0, 0))],
        out_specs=[row(512), row(512), row(128), row(128), row(1024)],
        out_shape=[jax.ShapeDtypeStruct((s, 512), F32), jax.ShapeDtypeStruct((s, 512), F32),
                   jax.ShapeDtypeStruct((s, 128), F32), jax.ShapeDtypeStruct((s, 128), F32),
                   jax.ShapeDtypeStruct((s, 1024), F32)],
        compiler_params=_cparams("parallel"),
        name="l0_in_proj",
    )(x, g, w)


def _mla_qkv_kernel(cq_ref, ckv_ref, kra_ref, krb_ref, cos_ref, sin_ref, gq_ref, gkv_ref,
                    wuq_ref, wuk_ref, wvt_ref, qn_ref, kn_ref, q_ref, k_ref, vt_ref):
    cos = cos_ref[...]
    sin = sin_ref[...]
    qn = qn_ref[...]
    kn = kn_ref[...]
    qp = _dot(_rms(cq_ref[...], gq_ref[...]).astype(BF16), wuq_ref[...])
    ckv_n = _rms(ckv_ref[...], gkv_ref[...]).astype(BF16)
    k_nopes = _dot(ckv_n, wuk_ref[...])
    v_t = _dot_nt(wvt_ref[...], ckv_n)
    k_rot = kra_ref[...] * cos + krb_ref[...] * sin
    k_rot_ss = jnp.sum(k_rot * k_rot, axis=-1, keepdims=True)
    q_scale = MLA_QK ** -0.5 * LOG2E
    for h in range(MLA_HEADS):
        qb = h * 384
        nope = qp[:, qb:qb + 128]
        rot = qp[:, qb + 128:qb + 256] * cos + qp[:, qb + 256:qb + 384] * sin
        ss = jnp.sum(nope * nope, axis=-1, keepdims=True) + jnp.sum(rot * rot, axis=-1, keepdims=True)
        inv = lax.rsqrt(ss * (1.0 / MLA_QK) + EPS) * q_scale
        q_ref[h, :, 0:128] = (nope * inv * qn[0:1, :]).astype(BF16)
        q_ref[h, :, 128:256] = (rot * inv * qn[1:2, :]).astype(BF16)
        k_nope = k_nopes[:, h * 128:(h + 1) * 128]
        ssk = jnp.sum(k_nope * k_nope, axis=-1, keepdims=True) + k_rot_ss
        invk = lax.rsqrt(ssk * (1.0 / MLA_QK) + EPS)
        k_ref[h, :, 0:128] = (k_nope * invk * kn[0:1, :]).astype(BF16)
        k_ref[h, :, 128:256] = (k_rot * invk * kn[1:2, :]).astype(BF16)
        vt_ref[h, 0:MLA_V, :] = v_t[h * MLA_V:(h + 1) * MLA_V, :].astype(BF16)
        vt_ref[h, MLA_V:VT_ROWS, :] = _ones_row_block(vt_ref.shape[-1])


def _ones_row_block(width):
    row = lax.broadcasted_iota(jnp.int32, (VT_ROWS - 128, width), 0)
    return jnp.where(row == 0, 1.0, 0.0).astype(BF16)


def _mla_qkv(cq, ckv, kra, krb, cos_t, sin_t, gq, gkv, wuq, wuk, wvt, qn, kn, tile):
    s = cq.shape[0]
    tm = min(256, s)
    per_tile = tile // tm
    row = lambda width: pl.BlockSpec((tm, width), lambda i: (i, 0))
    full = lambda a: pl.BlockSpec(a.shape, lambda i: (0,) * a.ndim)
    head_out = lambda width: pl.BlockSpec((MLA_HEADS, tm, width), lambda i: (0, i, 0))
    return pl.pallas_call(
        _mla_qkv_kernel,
        grid=(s // tm,),
        in_specs=[row(512), row(512), row(128), row(128), row(128), row(128),
                  full(gq), full(gkv), full(wuq), full(wuk), full(wvt), full(qn), full(kn)],
        out_specs=[head_out(HEAD_PAD), head_out(HEAD_PAD),
                   pl.BlockSpec((MLA_HEADS, None, VT_ROWS, tm), lambda i: (0, i // per_tile, 0, i % per_tile))],
        out_shape=[jax.ShapeDtypeStruct((MLA_HEADS, s, HEAD_PAD), BF16),
                   jax.ShapeDtypeStruct((MLA_HEADS, s, HEAD_PAD), BF16),
                   jax.ShapeDtypeStruct((MLA_HEADS, s // tile, VT_ROWS, tile), BF16)],
        compiler_params=_cparams("parallel"),
        name="mla_qkv",
    )(cq, ckv, kra, krb, cos_t, sin_t, gq, gkv, wuq, wuk, wvt, qn, kn)


def _flash_kernel(slope_ref, q_ref, k_ref, vt_ref, o_ref, m_ref, acc_ref, s_ref, cm_ref, *, tile, strip):
    h = pl.program_id(0)
    pair = pl.program_id(1)
    base = 2 * pair
    slope_tile = slope_ref[h] * tile
    m_ref[...] = jnp.full(m_ref.shape, -jnp.inf, F32)
    acc_ref[...] = jnp.zeros(acc_ref.shape, F32)
    n_strips = tile // strip
    strips = [(c, s) for c in range(2) for s in range(n_strips)]

    def cols(s):
        return slice(s * strip, (s + 1) * strip)

    def scores(c, s, kj, diagonal):
        q = q_ref[c * tile + s * strip:c * tile + (s + 1) * strip, :]
        st = _dot_nt(k_ref[kj], q)
        if diagonal:
            key_idx = lax.broadcasted_iota(jnp.int32, (tile, strip), 0)
            qry_idx = lax.broadcasted_iota(jnp.int32, (tile, strip), 1) + s * strip
            st = jnp.where(key_idx <= qry_idx, st, MASK_VALUE)
        return st, jnp.max(st, axis=0, keepdims=True)

    def softmax_pv(c, s, kj, st, col_max):
        shift = slope_tile * (kj - (base + c)).astype(F32)
        m_prev = m_ref[c, :, cols(s)]
        m_new = jnp.maximum(m_prev, col_max + shift)
        alpha = jnp.exp2(m_prev - m_new)
        p = jnp.exp2(st - (m_new - shift)).astype(BF16)
        acc_ref[c, :, cols(s)] = alpha * acc_ref[c, :, cols(s)] + _dot(vt_ref[kj], p)
        m_ref[c, :, cols(s)] = m_new

    def hold(c, s, st, col_max):
        s_ref[c, :, cols(s)] = st
        cm_ref[c, :, cols(s)] = col_max

    def held(c, s):
        return s_ref[c, :, cols(s)], cm_ref[c, :, cols(s)]

    pending = None
    for s in range(n_strips):
        current = scores(1, s, base, False)
        if pending is not None:
            softmax_pv(1, s - 1, base, *pending)
        pending = current
    for i, (c, s) in enumerate(strips):
        diag = scores(c, s, base + c, True)
        if i == 0:
            softmax_pv(1, n_strips - 1, base, *pending)
        hold(c, s, *diag)

    def body(j, carry):
        for c, s in strips:
            nxt = scores(c, s, j, False)
            softmax_pv(c, s, jnp.where(j == 0, base + c, j - 1), *held(c, s))
            hold(c, s, *nxt)
        return carry

    lax.fori_loop(0, base, body, 0)
    for c, s in strips:
        softmax_pv(c, s, jnp.where(base == 0, base + c, base - 1), *held(c, s))
    for c in range(2):
        acc = acc_ref[c]
        out_t = acc[0:128, :] / acc[128:129, :]
        o_ref[c * tile:(c + 1) * tile, :] = out_t.T.astype(o_ref.dtype)


def _flash(q, k, vt, slopes, *, tile, name):
    heads, s, _ = q.shape
    nk = s // tile
    return pl.pallas_call(
        functools.partial(_flash_kernel, tile=tile, strip=min(FLASH_STRIP, tile)),
        grid_spec=pltpu.PrefetchScalarGridSpec(
            num_scalar_prefetch=1,
            grid=(heads, s // (2 * tile)),
            in_specs=[pl.BlockSpec((None, 2 * tile, HEAD_PAD), lambda h, i, sl: (h, i, 0)),
                      pl.BlockSpec((None, nk, tile, HEAD_PAD), lambda h, i, sl: (h, 0, 0, 0)),
                      pl.BlockSpec((None, nk, VT_ROWS, tile), lambda h, i, sl: (h, 0, 0, 0))],
            out_specs=pl.BlockSpec((2 * tile, 128), lambda h, i, sl: (i, h)),
            scratch_shapes=[pltpu.VMEM((2, 1, tile), F32), pltpu.VMEM((2, VT_ROWS, tile), F32),
                            pltpu.VMEM((2, tile, tile), F32), pltpu.VMEM((2, 1, tile), F32)],
        ),
        out_shape=jax.ShapeDtypeStruct((s, heads * 128), BF16),
        compiler_params=_cparams("parallel", "arbitrary"),
        name=name,
    )(slopes, q, k, vt)


def _s5_kernel(u_ref, lag_ref, wx_ref, vo_ref, a_ref, d_ref, glu_ref, o_ref, carry_ref, x_ref, s_ref, m_ref):
    cb = pl.program_id(1)
    n_chunks = u_ref.shape[0]
    half = S5_OCTET * S5_STATE

    @pl.when(cb == 0)
    def _():
        carry_ref[...] = jnp.zeros(carry_ref.shape, F32)
        for j in range(S5_CHUNK):
            for t in range(j, S5_CHUNK):
                m_ref[j * LANES:(j + 1) * LANES, t * LANES:(t + 1) * LANES] = lag_ref[t - j]
        for t in range(0, S5_CHUNK, 2):
            m_ref[(t + 1) * LANES:(t + 2) * LANES, t * LANES:(t + 1) * LANES] = jnp.zeros((LANES, LANES), BF16)

    steps = [u_ref[:, j, :] for j in range(S5_CHUNK)]
    xb = jnp.concatenate([p.astype(BF16) for p in steps], axis=1)
    x_ref[...] = _dot(xb, wx_ref[...])
    a_re = a_ref[0:1, :]
    a_im = a_ref[1:2, :]

    def body(c, carry):
        re, im = carry
        s_ref[pl.ds(c, 1), :] = jnp.concatenate([re, im], axis=1)
        x = x_ref[pl.ds(c, 1), :]
        return (a_re * re - a_im * im + x[:, :half], a_re * im + a_im * re + x[:, half:])

    re, im = lax.fori_loop(0, n_chunks, body, (carry_ref[0:1, :], carry_ref[1:2, :]), unroll=8)
    carry_ref[0:1, :] = re
    carry_ref[1:2, :] = im

    sb = s_ref[...].astype(BF16)
    d = d_ref[...]
    for tp in range(S5_CHUNK // 2):
        lo, hi = 2 * tp * LANES, (2 * tp + 2) * LANES
        y2 = _dot(xb[:, :hi], m_ref[0:hi, lo:hi]) + _dot(sb, vo_ref[:, lo:hi])
        ys = [jax.nn.gelu(y2[:, k * LANES:(k + 1) * LANES] + d * steps[2 * tp + k]).astype(BF16)
              for k in range(2)]
        vg = _dot(jnp.concatenate(ys, axis=0), glu_ref[...])
        out = vg[:, :LANES] * jax.nn.sigmoid(vg[:, LANES:])
        for k in range(2):
            o_ref[:, 2 * tp + k, :] = out[k * n_chunks:(k + 1) * n_chunks, :]


def _s5(u, lags, wx, vo, a, d, glu):
    s = u.shape[0]
    n_chunks = s // S5_CHUNK
    cb = min(S5_CHUNK_BLOCK, n_chunks)
    octets = S5_GROUPS // S5_OCTET
    u3 = u.reshape(n_chunks, S5_CHUNK, S5_WIDTH)
    per_octet = lambda arr: pl.BlockSpec((None,) + arr.shape[1:], lambda o, i: (o,) + (0,) * (arr.ndim - 1),
                                         pipeline_mode=pl.Buffered(1))
    io_spec = pl.BlockSpec((cb, S5_CHUNK, LANES), lambda o, i: (i, 0, o))
    state_w = 2 * S5_OCTET * S5_STATE
    y3 = pl.pallas_call(
        _s5_kernel,
        grid=(octets, n_chunks // cb),
        in_specs=[io_spec] + [per_octet(arr) for arr in (lags, wx, vo, a, d, glu)],
        out_specs=io_spec,
        out_shape=jax.ShapeDtypeStruct((n_chunks, S5_CHUNK, S5_WIDTH), F32),
        scratch_shapes=[pltpu.VMEM((2, state_w // 2), F32), pltpu.VMEM((cb, state_w), F32),
                        pltpu.VMEM((cb, state_w), F32),
                        pltpu.VMEM((S5_CHUNK * LANES, S5_CHUNK * LANES), BF16)],
        compiler_params=_cparams("parallel", "arbitrary"),
        name="s5_scan",
    )(u3, lags, wx, vo, a, d, glu)
    return y3.reshape(s, S5_WIDTH)


def _s5_operators(a_re, a_im, log_dt, b_re, b_im, c_re, c_im, d_skip, glu_val, glu_gate):
    hp = lax.Precision.HIGHEST
    a_re, a_im = a_re.astype(F32), a_im.astype(F32)
    dt = jnp.exp(log_dt.astype(F32))[:, None]
    er, ei = a_re * dt, a_im * dt

    def lam_pow(k):
        kk = k.astype(F32)[:, None, None]
        mag = jnp.exp(er[None] * kk)
        return mag * jnp.cos(ei[None] * kk), mag * jnp.sin(ei[None] * kk)

    l1r, l1i = lam_pow(jnp.arange(1, 2))
    nr, ni = l1r[0] - 1.0, l1i[0]
    den = a_re * a_re + a_im * a_im
    fr, fi = (nr * a_re + ni * a_im) / den, (ni * a_re - nr * a_im) / den
    b_re, b_im = b_re.astype(F32), b_im.astype(F32)
    bbr = fr[:, :, None] * b_re - fi[:, :, None] * b_im
    bbi = fr[:, :, None] * b_im + fi[:, :, None] * b_re
    c_re, c_im = c_re.astype(F32), c_im.astype(F32)

    L, A, N, P = S5_CHUNK, S5_OCTET, S5_STATE, S5_GROUP
    O = S5_GROUPS // A
    same = jnp.eye(A, dtype=bool)
    pr, pi = lam_pow(jnp.arange(0, L + 1))
    lbr = pr[:L, :, :, None] * bbr[None] - pi[:L, :, :, None] * bbi[None]
    lbi = pr[:L, :, :, None] * bbi[None] + pi[:L, :, :, None] * bbr[None]
    kern = (jnp.einsum('gpn,lgnq->lgqp', c_re, lbr, precision=hp)
            - jnp.einsum('gpn,lgnq->lgqp', c_im, lbi, precision=hp))
    lags = kern.reshape(L, O, A, P, P).transpose(1, 0, 2, 3, 4)
    lags = jnp.where(same[None, None, :, None, :, None], lags[:, :, :, :, None, :], 0.0)
    lags = lags.astype(BF16).reshape(O, L, A * P, A * P)
    rev = L - 1 - jnp.arange(L)

    def state_cols(w):
        w = w.reshape(L, O, A, N, P).transpose(1, 0, 2, 4, 3)
        w = jnp.where(same[None, None, :, None, :, None], w[:, :, :, :, None, :], 0.0)
        return w.astype(BF16).reshape(O, L * A * P, A * N)

    wx = jnp.concatenate([state_cols(lbr[rev]), state_cols(lbi[rev])], axis=-1)
    p1r, p1i = pr[1:], pi[1:]
    cr = c_re.transpose(0, 2, 1)[None]
    ci = c_im.transpose(0, 2, 1)[None]
    vr = cr * p1r[..., None] - ci * p1i[..., None]
    vi = cr * p1i[..., None] + ci * p1r[..., None]

    def state_rows(v):
        v = v.reshape(L, O, A, N, P).transpose(1, 2, 3, 0, 4)
        v = jnp.where(same[None, :, None, None, :, None], v[:, :, :, :, None, :], 0.0)
        return v.astype(BF16).reshape(O, A * N, L * A * P)

    vo = jnp.concatenate([state_rows(vr), state_rows(-vi)], axis=1)
    a = jnp.stack([pr[L].reshape(O, A * N), pi[L].reshape(O, A * N)], axis=1)
    d = d_skip.astype(F32).reshape(O, 1, A * P)

    def glu_cols(w):
        w = w.astype(F32).reshape(O, A, P, P)
        w = jnp.where(same[None, :, None, :, None], w[:, :, :, None, :], 0.0)
        return w.astype(BF16).reshape(O, A * P, A * P)

    glu = jnp.concatenate([glu_cols(glu_val), glu_cols(glu_gate)], axis=-1)
    return lags, wx, vo, a, d, glu


def _proj_residual_kernel(*refs, n_in):
    x_ref = refs[0]
    o_ref = refs[1 + 2 * n_in]
    acc = x_ref[...]
    for i in range(n_in):
        acc = acc + _dot(refs[1 + i][...].astype(BF16), refs[1 + n_in + i][...])
    o_ref[...] = acc


def _proj_residual(x, acts, weights):
    s = x.shape[0]
    tm = min(512, s)
    n_in = len(acts)
    return pl.pallas_call(
        functools.partial(_proj_residual_kernel, n_in=n_in),
        grid=(s // tm,),
        in_specs=([pl.BlockSpec((tm, D_MODEL), lambda i: (i, 0))]
                  + [pl.BlockSpec((tm, a.shape[1]), lambda i: (i, 0)) for a in acts]
                  + [pl.BlockSpec(w.shape, lambda i: (0, 0)) for w in weights]),
        out_specs=pl.BlockSpec((tm, D_MODEL), lambda i: (i, 0)),
        out_shape=jax.ShapeDtypeStruct((s, D_MODEL), F32),
        compiler_params=_cparams("parallel"),
        name="proj_residual",
    )(x, *acts, *weights)


def _mem_kv_kernel(mem_ref, g_ref, w_ref, kn_ref, k_ref, v_ref):
    kv = _dot(_rms(mem_ref[...], g_ref[...]).astype(BF16), w_ref[...])
    width = XA_HEADS * XA_HD
    for h in range(XA_HEADS):
        k_ref[h] = _rms(kv[:, h * XA_HD:(h + 1) * XA_HD], kn_ref[...]).astype(BF16)
        v_ref[h] = kv[:, width + h * XA_HD:width + (h + 1) * XA_HD].astype(BF16)


def _mem_kv(mem, g, w, kn):
    m_len = mem.shape[0]
    shape = jax.ShapeDtypeStruct((XA_HEADS, m_len, XA_HD), BF16)
    return pl.pallas_call(
        _mem_kv_kernel,
        out_shape=[shape, shape],
        compiler_params=pltpu.CompilerParams(vmem_limit_bytes=VMEM_LIMIT_BYTES),
        name="xa_mem_kv",
    )(mem, g, w, kn)


def _xa_kernel(x_ref, g_ref, wq_ref, qn_ref, k_ref, v_ref, wo_ref, o_ref):
    x = x_ref[...]
    q = _dot(_rms(x, g_ref[...]).astype(BF16), wq_ref[...])
    qn = qn_ref[...] * (XA_HD ** -0.5)
    heads = []
    for h in range(XA_HEADS):
        qh = _rms(q[:, h * XA_HD:(h + 1) * XA_HD], qn).astype(BF16)
        sc = _dot_nt(qh, k_ref[h])
        sc = sc - jnp.max(sc, axis=-1, keepdims=True)
        p = jnp.exp(sc)
        p = p / jnp.sum(p, axis=-1, keepdims=True)
        heads.append(_dot(p.astype(BF16), v_ref[h]).astype(BF16))
    o = jnp.concatenate(heads, axis=-1)
    o_ref[...] = x + _dot(o, wo_ref[...])


def _xa(x, g, wq, qn, k, v, wo):
    s = x.shape[0]
    tm = min(512, s)
    full = lambda a: pl.BlockSpec(a.shape, lambda i: (0,) * a.ndim)
    return pl.pallas_call(
        _xa_kernel,
        grid=(s // tm,),
        in_specs=[pl.BlockSpec((tm, D_MODEL), lambda i: (i, 0)), full(g), full(wq), full(qn),
                  full(k), full(v), full(wo)],
        out_specs=pl.BlockSpec((tm, D_MODEL), lambda i: (i, 0)),
        out_shape=jax.ShapeDtypeStruct((s, D_MODEL), F32),
        compiler_params=_cparams("parallel"),
        name="xa",
    )(x, g, wq, qn, k, v, wo)


def _mlp_kernel(x_ref, g_ref, wu_ref, wd_ref, o_ref, xn_ref, *, sub):
    j = pl.program_id(1)

    @pl.when(j == 0)
    def _():
        x = x_ref[...]
        xn_ref[...] = _rms(x, g_ref[...]).astype(BF16)
        o_ref[...] = x

    xn = xn_ref[...]
    n_sub = wu_ref.shape[1] // sub
    hids = [_dot(xn, wu_ref[:, i * sub:(i + 1) * sub]) for i in range(n_sub)]
    acc = o_ref[...]
    for i in range(n_sub):
        hid = jnp.maximum(hids[i], 0.0)
        acc = acc + _dot((hid * hid).astype(BF16), wd_ref[i * sub:(i + 1) * sub, :])
    o_ref[...] = acc


def _mlp(x, g, wu, wd):
    s = x.shape[0]
    tm = min(512, s)
    tf = 2048
    return pl.pallas_call(
        functools.partial(_mlp_kernel, sub=512),
        grid=(s // tm, D_FF // tf),
        in_specs=[pl.BlockSpec((tm, D_MODEL), lambda i, j: (i, 0)),
                  pl.BlockSpec((1, D_MODEL), lambda i, j: (0, 0)),
                  pl.BlockSpec((D_MODEL, tf), lambda i, j: (0, j)),
                  pl.BlockSpec((tf, D_MODEL), lambda i, j: (j, 0))],
        out_specs=pl.BlockSpec((tm, D_MODEL), lambda i, j: (i, 0)),
        out_shape=jax.ShapeDtypeStruct((s, D_MODEL), F32),
        scratch_shapes=[pltpu.VMEM((tm, D_MODEL), BF16)],
        compiler_params=_cparams("parallel", "arbitrary"),
        name="mlp",
    )(x, g, wu, wd)


def _moba_qkv_kernel(x_ref, g_ref, w_ref, wvt_ref, hn_ref, qk_ref, vt_ref, xn_ref, *,
                     heads_per_block, norm_blocks, row_split):
    j = pl.program_id(1)

    @pl.when(j == 0)
    def _():
        xn_ref[...] = _rms(x_ref[...], g_ref[...]).astype(BF16)

    @pl.when(j < norm_blocks)
    def _():
        rows = xn_ref.shape[0] // row_split
        products = [_dot(xn_ref[r * rows:(r + 1) * rows, :], w_ref[...]) for r in range(row_split)]
        for r, p in enumerate(products):
            for h in range(heads_per_block):
                sl = slice(h * MOBA_HD, (h + 1) * MOBA_HD)
                qk_ref[r * rows:(r + 1) * rows, sl] = _rms(p[:, sl], hn_ref[:, sl]).astype(qk_ref.dtype)

    @pl.when(j >= norm_blocks)
    def _():
        v_t = _dot_nt(wvt_ref[...], xn_ref[...])
        for h in range(heads_per_block):
            vt_ref[h, 0:MOBA_HD, :] = v_t[h * MOBA_HD:(h + 1) * MOBA_HD, :].astype(vt_ref.dtype)
            vt_ref[h, MOBA_HD:VT_ROWS, :] = _ones_row_block(vt_ref.shape[-1])


def _moba_qkv(x, g, w_qk, w_vt, head_gains, tile):
    s = x.shape[0]
    tm = tile
    tn = 512
    heads_per_block = tn // MOBA_HD
    norm_blocks = w_qk.shape[1] // tn
    v_blocks = w_vt.shape[0] // tn
    kernel = functools.partial(_moba_qkv_kernel, heads_per_block=heads_per_block, norm_blocks=norm_blocks,
                               row_split=4 if tm % 1024 == 0 else 1)
    qk_blk = lambda j: jnp.minimum(j, norm_blocks - 1)
    v_blk = lambda j: jnp.maximum(j - norm_blocks, 0)
    return pl.pallas_call(
        kernel,
        grid=(s // tm, norm_blocks + v_blocks),
        in_specs=[pl.BlockSpec((tm, D_MODEL), lambda i, j: (i, 0)),
                  pl.BlockSpec((1, D_MODEL), lambda i, j: (0, 0)),
                  pl.BlockSpec((D_MODEL, tn), lambda i, j: (0, qk_blk(j))),
                  pl.BlockSpec((tn, D_MODEL), lambda i, j: (v_blk(j), 0)),
                  pl.BlockSpec((1, tn), lambda i, j: (0, qk_blk(j)))],
        out_specs=[pl.BlockSpec((tm, tn), lambda i, j: (i, qk_blk(j))),
                   pl.BlockSpec((heads_per_block, None, VT_ROWS, tm), lambda i, j: (v_blk(j), i, 0, 0))],
        out_shape=[jax.ShapeDtypeStruct((s, w_qk.shape[1]), BF16),
                   jax.ShapeDtypeStruct((MOBA_HEADS, s // tile, VT_ROWS, tile), BF16)],
        scratch_shapes=[pltpu.VMEM((tm, D_MODEL), BF16)],
        compiler_params=_cparams("parallel", "arbitrary"),
        name="moba_qkv",
    )(x, g, w_qk, w_vt, head_gains)


def _kmean_kernel(k_ref, o_ref, *, blocks):
    for b in range(blocks):
        kb = k_ref[b * MOBA_BLOCK:(b + 1) * MOBA_BLOCK, :].astype(F32)
        o_ref[b:b + 1, :] = jnp.mean(kb, axis=0, keepdims=True)


def _kmean(qk):
    s = qk.shape[0]
    width = MOBA_HEADS * MOBA_HD
    n_blk = s // MOBA_BLOCK
    blocks = 8 if n_blk % 8 == 0 else n_blk
    rows = blocks * MOBA_BLOCK
    return pl.pallas_call(
        functools.partial(_kmean_kernel, blocks=blocks),
        grid=(s // rows, width // 512),
        in_specs=[pl.BlockSpec((rows, 512), lambda i, j: (i, width // 512 + j))],
        out_specs=pl.BlockSpec((blocks, 512), lambda i, j: (i, j)),
        out_shape=jax.ShapeDtypeStruct((n_blk, width), F32),
        compiler_params=_cparams("parallel", "parallel"),
        name="moba_kmean",
    )(qk)


def _split3(x):
    hi = x.astype(BF16)
    r1 = x - hi.astype(F32)
    mid = r1.astype(BF16)
    lo = (r1 - mid.astype(F32)).astype(BF16)
    return hi.astype(F32), mid.astype(F32), lo.astype(F32)


ROUTE_LANES = 64
SLOPE_RC_LANE = 64
SLOPE_RB_LANE = 67
QUERY_BIAS_LANE = 70


def _moba_route_kernel(q_ref, k_ref, km_ref, slope_ref, slope_cols_ref, qa_ref, ka_ref, *, blocks_per_tile):
    i = pl.program_id(0)
    shape = (MOBA_BLOCK, LANES)
    lane = lax.broadcasted_iota(jnp.int32, shape, 1)
    row_f = lax.broadcasted_iota(jnp.int32, shape, 0).astype(F32)
    blk_in_tile = (i % blocks_per_tile).astype(F32)
    offset_in_tile = blk_in_tile * MOBA_BLOCK + row_f
    k_extra = jnp.where(lane == i, 1.0, 0.0)
    k_extra = jnp.where((lane >= SLOPE_RC_LANE) & (lane < SLOPE_RC_LANE + 3), row_f, k_extra)
    k_extra = jnp.where((lane >= SLOPE_RB_LANE) & (lane < SLOPE_RB_LANE + 3), blk_in_tile, k_extra)
    k_extra = jnp.where((lane >= QUERY_BIAS_LANE) & (lane < QUERY_BIAS_LANE + 3), 1.0, k_extra)
    k_extra = k_extra.astype(BF16)
    blk = lax.broadcasted_iota(jnp.int32, (LANES, MOBA_BLOCK), 0)
    blk_f = blk.astype(F32)
    for h in range(MOBA_HEADS):
        sl = slice(h * MOBA_HD, (h + 1) * MOBA_HD)
        q = q_ref[:, sl]
        gate = lax.dot_general(km_ref[h], q.astype(F32), (((1,), (1,)), ((), ())),
                               precision=lax.Precision.HIGHEST, preferred_element_type=F32)
        gate = jnp.where(blk < i, gate, -jnp.inf)
        chosen = blk == i
        for r in range(MOBA_TOPK):
            best = jnp.max(gate, axis=0, keepdims=True)
            idx = jnp.min(jnp.where(gate == best, blk_f, float(LANES)), axis=0, keepdims=True)
            pick = blk_f == idx
            chosen = jnp.logical_or(chosen, jnp.logical_and(pick, r < i))
            gate = jnp.where(pick, -jnp.inf, gate)
        q_extra = jnp.where(chosen, 0.0, MASK_VALUE).T
        q_extra = jnp.where(lane < ROUTE_LANES, q_extra, slope_cols_ref[h])
        bias = _split3(-slope_ref[h] * offset_in_tile)
        for piece in range(3):
            q_extra = jnp.where(lane == QUERY_BIAS_LANE + piece, bias[piece], q_extra)
        qa_ref[h, :, 0:128] = q
        qa_ref[h, :, 128:256] = q_extra.astype(BF16)
        ka_ref[h, :, 0:128] = k_ref[:, sl]
        ka_ref[h, :, 128:256] = k_extra


def _moba_route(qk, kmean_t, slopes, tile):
    s = qk.shape[0]
    n_blk = s // MOBA_BLOCK
    width = MOBA_HEADS * MOBA_HD
    aug = jax.ShapeDtypeStruct((MOBA_HEADS, s, HEAD_PAD), BF16)
    pieces = _split3(slopes)
    lane = jnp.arange(LANES)
    cols = jnp.zeros((MOBA_HEADS, LANES), F32)
    for piece in range(3):
        cols = jnp.where(lane[None, :] == SLOPE_RC_LANE + piece, pieces[piece][:, None], cols)
        cols = jnp.where(lane[None, :] == SLOPE_RB_LANE + piece, MOBA_BLOCK * pieces[piece][:, None], cols)
    slope_rows = jnp.broadcast_to(slopes[:, None, None], (MOBA_HEADS, 1, LANES))
    full = lambda a: pl.BlockSpec(a.shape, lambda i: (0,) * a.ndim)
    return pl.pallas_call(
        functools.partial(_moba_route_kernel, blocks_per_tile=tile // MOBA_BLOCK),
        grid=(n_blk,),
        in_specs=[pl.BlockSpec((MOBA_BLOCK, width), lambda i: (i, 0)),
                  pl.BlockSpec((MOBA_BLOCK, width), lambda i: (i, 1)),
                  full(kmean_t), full(slope_rows), full(cols[:, None, :])],
        out_specs=[pl.BlockSpec((MOBA_HEADS, MOBA_BLOCK, HEAD_PAD), lambda i: (0, i, 0))] * 2,
        out_shape=[aug, aug],
        compiler_params=_cparams("parallel"),
        name="moba_route",
    )(qk, qk, kmean_t, slope_rows, cols[:, None, :])


def _rope_tables(s):
    inv_freq = 1.0 / (ROPE_BASE ** (jnp.arange(0, MLA_ROPE, 2, dtype=F32) / MLA_ROPE))
    ang = jnp.arange(s, dtype=F32)[:, None] * inv_freq[None, :]
    cos, sin = jnp.cos(ang), jnp.sin(ang)
    zero = jnp.zeros((s, 64), F32)
    return jnp.concatenate([cos, cos, zero], axis=1), jnp.concatenate([-sin, sin, zero], axis=1)


def _rope_cols(w):
    x1, x2 = w[..., :32], w[..., 32:]
    zero = jnp.zeros(w.shape[:-1] + (64,), w.dtype)
    return jnp.concatenate([x1, x2, zero], axis=-1), jnp.concatenate([x2, x1, zero], axis=-1)


def _row(v):
    return v.astype(F32).reshape(1, -1)


def _qk_gain_rows(g):
    g = g.astype(F32)
    return jnp.stack([g[:MLA_NOPE], jnp.concatenate([g[MLA_NOPE:], jnp.zeros((64,), F32)])])


def _cross_attention_mlp(x, mem, xa_norm, xa_mem_norm, xa_w_q, xa_w_kv, xa_q_norm, xa_k_norm, xa_w_o,
                         mlp_norm, w_up, w_down):
    k, v = _mem_kv(mem, _row(xa_mem_norm), xa_w_kv.astype(BF16), _row(xa_k_norm))
    x = _xa(x, _row(xa_norm), xa_w_q.astype(BF16), _row(xa_q_norm), k, v, xa_w_o.astype(BF16))
    return _mlp(x, _row(mlp_norm), w_up.astype(BF16), w_down.astype(BF16))


def kernel(x, mem, l0_mix_norm, l0_w_in, l0_q_lora_norm, l0_w_uq, l0_kv_lora_norm, l0_w_ukv, l0_mla_q_norm, l0_mla_k_norm, l0_s5_a_re, l0_s5_a_im, l0_s5_log_dt, l0_s5_b_re, l0_s5_b_im, l0_s5_c_re, l0_s5_c_im, l0_s5_d, l0_s5_glu_val, l0_s5_glu_gate, l0_w_out, l0_xa_norm, l0_xa_mem_norm, l0_xa_w_q, l0_xa_w_kv, l0_xa_q_norm, l0_xa_k_norm, l0_xa_w_o, l0_mlp_norm, l0_w_up, l0_w_down, l1_mix_norm, l1_w_qkv, l1_q_norm, l1_k_norm, l1_w_o, l1_xa_norm, l1_xa_mem_norm, l1_xa_w_q, l1_xa_w_kv, l1_xa_q_norm, l1_xa_k_norm, l1_xa_w_o, l1_mlp_norm, l1_w_up, l1_w_down):
    batch, s, _ = x.shape
    assert batch == 1 and s % MOBA_BLOCK == 0
    x = x.reshape(s, D_MODEL)
    mem = mem.reshape(mem.shape[1], D_MODEL)
    tile = min(FLASH_TILE, s // 2)
    no_slopes = jnp.zeros((MLA_HEADS,), F32)

    w_cq = l0_w_in[:, :Q_LORA]
    w_ckv = l0_w_in[:, Q_LORA:Q_LORA + KV_LORA]
    w_kra, w_krb = _rope_cols(l0_w_in[:, Q_LORA + KV_LORA:Q_LORA + KV_LORA + MLA_ROPE])
    w_u = l0_w_in[:, Q_LORA + KV_LORA + MLA_ROPE:]
    w_in = jnp.concatenate([w_cq, w_ckv, w_kra, w_krb, w_u], axis=1).astype(BF16)
    cq, ckv, kra, krb, u = _l0_in(x, _row(l0_mix_norm), w_in)

    wuq = l0_w_uq.reshape(Q_LORA, MLA_HEADS, MLA_QK)
    wuq_a, wuq_b = _rope_cols(wuq[:, :, MLA_NOPE:])
    wuq = jnp.concatenate([wuq[:, :, :MLA_NOPE], wuq_a, wuq_b], axis=-1).reshape(Q_LORA, MLA_HEADS * 384)
    cos_t, sin_t = _rope_tables(s)
    wukv = l0_w_ukv.reshape(KV_LORA, MLA_HEADS, MLA_NOPE + MLA_V)
    wuk = wukv[:, :, :MLA_NOPE].reshape(KV_LORA, MLA_HEADS * MLA_NOPE)
    wvt = wukv[:, :, MLA_NOPE:].reshape(KV_LORA, MLA_HEADS * MLA_V).T
    q, k, vt = _mla_qkv(cq, ckv, kra, krb, cos_t, sin_t, _row(l0_q_lora_norm), _row(l0_kv_lora_norm),
                        wuq.astype(BF16), wuk.astype(BF16), wvt.astype(BF16),
                        _qk_gain_rows(l0_mla_q_norm), _qk_gain_rows(l0_mla_k_norm), tile)
    kb = k.reshape(MLA_HEADS, s // tile, tile, HEAD_PAD)
    y_a = _flash(q, kb, vt, no_slopes, tile=tile, name="flash_mla")

    ops = _s5_operators(l0_s5_a_re, l0_s5_a_im, l0_s5_log_dt, l0_s5_b_re, l0_s5_b_im,
                        l0_s5_c_re, l0_s5_c_im, l0_s5_d, l0_s5_glu_val, l0_s5_glu_gate)
    y_b = _s5(u, *ops)

    w_out = l0_w_out.astype(BF16)
    x = _proj_residual(x, [y_a, y_b], [w_out[:MLA_HEADS * MLA_V], w_out[MLA_HEADS * MLA_V:]])
    x = _cross_attention_mlp(x, mem, l0_xa_norm, l0_xa_mem_norm, l0_xa_w_q, l0_xa_w_kv, l0_xa_q_norm,
                             l0_xa_k_norm, l0_xa_w_o, l0_mlp_norm, l0_w_up, l0_w_down)

    width = MOBA_HEADS * MOBA_HD
    head_gains = jnp.concatenate([jnp.tile(l1_q_norm.astype(F32) * (MOBA_HD ** -0.5 * LOG2E), MOBA_HEADS),
                                  jnp.tile(l1_k_norm.astype(F32), MOBA_HEADS)]).reshape(1, 2 * width)
    qk, vt1 = _moba_qkv(x, _row(l1_mix_norm), l1_w_qkv[:, :2 * width].astype(BF16),
                        l1_w_qkv[:, 2 * width:].T.astype(BF16), head_gains, tile)
    kmean = _kmean(qk)
    n_blk = s // MOBA_BLOCK
    kmean_t = kmean.reshape(n_blk, MOBA_HEADS, MOBA_HD).transpose(1, 0, 2)
    kmean_t = jnp.pad(kmean_t, ((0, 0), (0, LANES - n_blk), (0, 0)))
    slopes = LOG2E * 2.0 ** (-8.0 * jnp.arange(1, MOBA_HEADS + 1, dtype=F32) / MOBA_HEADS)
    q_aug, k_aug = _moba_route(qk, kmean_t, slopes, tile)
    kb1 = k_aug.reshape(MOBA_HEADS, s // tile, tile, HEAD_PAD)
    o1 = _flash(q_aug, kb1, vt1, slopes, tile=tile, name="flash_moba")
    x = _proj_residual(x, [o1], [l1_w_o.astype(BF16)])
    x = _cross_attention_mlp(x, mem, l1_xa_norm, l1_xa_mem_norm, l1_xa_w_q, l1_xa_w_kv, l1_xa_q_norm,
                             l1_xa_k_norm, l1_xa_w_o, l1_mlp_norm, l1_w_up, l1_w_down)
    return x.reshape(1, s, D_MODEL)
```

```python
import functools
import math

import jax
import jax.numpy as jnp
from jax import lax
from jax.experimental import pallas as pl
from jax.experimental.pallas import tpu as pltpu

F32 = jnp.float32
BF16 = jnp.bfloat16

D_MODEL = 2048
EPS = 1e-6
MASK_VALUE = -1e30

MLA_HEADS = 8
MLA_NOPE = 128
MLA_ROPE = 64
MLA_QK = MLA_NOPE + MLA_ROPE
MLA_V = 128
Q_LORA = 512
KV_LORA = 512
ROPE_BASE = 10000.0
S5_WIDTH = 1024
S5_GROUP = 16
S5_GROUPS = 64
S5_STATE = 64
S5_CHUNK = 16
S5_OCTET = 8
S5_CHUNK_BLOCK = 256
S5_OUT_STEPS = 4
MOBA_HEADS = 16
MOBA_HD = 128
MOBA_BLOCK = 256
MOBA_TOPK = 3
XA_HEADS = 4
XA_HD = 128
D_FF = 4 * D_MODEL

LANES = 128
LOG2E = math.log2(math.e)
FLASH_TILE = 1024
FLASH_STRIP = 256
HEAD_PAD = 256
VT_ROWS = 136
VMEM_LIMIT_BYTES = 56 * 1024 * 1024


def _cparams(*sem):
    return pltpu.CompilerParams(dimension_semantics=sem, vmem_limit_bytes=VMEM_LIMIT_BYTES)


def _rms(x, g):
    return x * lax.rsqrt(jnp.mean(x * x, axis=-1, keepdims=True) + EPS) * g


def _dot(a, b):
    return jnp.dot(a, b, preferred_element_type=F32)


def _dot_nt(a, b):
    return lax.dot_general(a, b, (((1,), (1,)), ((), ())), preferred_element_type=F32)


def _l0_in_kernel(x_ref, g_ref, w_ref, cq_ref, ckv_ref, kra_ref, krb_ref, u_ref):
    xn = _rms(x_ref[...], g_ref[...]).astype(BF16)
    p = _dot(xn, w_ref[...])
    cq_ref[...] = p[:, 0:512]
    ckv_ref[...] = p[:, 512:1024]
    kra_ref[...] = p[:, 1024:1152]
    krb_ref[...] = p[:, 1152:1280]
    u_ref[...] = p[:, 1280:2304]


def _l0_in(x, g, w):
    s = x.shape[0]
    tm = min(512, s)
    n = w.shape[1]
    row = lambda width: pl.BlockSpec((tm, width), lambda i: (i, 0))
    return pl.pallas_call(
        _l0_in_kernel,
        grid=(s // tm,),
        in_specs=[row(D_MODEL), pl.BlockSpec((1, D_MODEL), lambda i: (0, 0)),
                  pl.BlockSpec((D_MODEL, n), lambda i: (0, 0))],
        out_specs=[row(512), row(512), row(128), row(128), row(1024)],
        out_shape=[jax.ShapeDtypeStruct((s, 512), F32), jax.ShapeDtypeStruct((s, 512), F32),
                   jax.ShapeDtypeStruct((s, 128), F32), jax.ShapeDtypeStruct((s, 128), F32),
                   jax.ShapeDtypeStruct((s, 1024), F32)],
        compiler_params=_cparams("parallel"),
        name="l0_in_proj",
    )(x, g, w)


def _mla_qkv_kernel(cq_ref, ckv_ref, kra_ref, krb_ref, cos_ref, sin_ref, gq_ref, gkv_ref,
                    wuq_ref, wuk_ref, wvt_ref, qn_ref, kn_ref, q_ref, k_ref, vt_ref):
    cos = cos_ref[...]
    sin = sin_ref[...]
    qn = qn_ref[...]
    kn = kn_ref[...]
    qp = _dot(_rms(cq_ref[...], gq_ref[...]).astype(BF16), wuq_ref[...])
    ckv_n = _rms(ckv_ref[...], gkv_ref[...]).astype(BF16)
    k_nopes = _dot(ckv_n, wuk_ref[...])
    v_t = _dot_nt(wvt_ref[...], ckv_n)
    k_rot = kra_ref[...] * cos + krb_ref[...] * sin
    k_rot_ss = jnp.sum(k_rot * k_rot, axis=-1, keepdims=True)
    q_scale = MLA_QK ** -0.5 * LOG2E
    for h in range(MLA_HEADS):
        qb = h * 384
        nope = qp[:, qb:qb + 128]
        rot = qp[:, qb + 128:qb + 256] * cos + qp[:, qb + 256:qb + 384] * sin
        ss = jnp.sum(nope * nope, axis=-1, keepdims=True) + jnp.sum(rot * rot, axis=-1, keepdims=True)
        inv = lax.rsqrt(ss * (1.0 / MLA_QK) + EPS) * q_scale
        q_ref[h, :, 0:128] = (nope * inv * qn[0:1, :]).astype(BF16)
        q_ref[h, :, 128:256] = (rot * inv * qn[1:2, :]).astype(BF16)
        k_nope = k_nopes[:, h * 128:(h + 1) * 128]
        ssk = jnp.sum(k_nope * k_nope, axis=-1, keepdims=True) + k_rot_ss
        invk = lax.rsqrt(ssk * (1.0 / MLA_QK) + EPS)
        k_ref[h, :, 0:128] = (k_nope * invk * kn[0:1, :]).astype(BF16)
        k_ref[h, :, 128:256] = (k_rot * invk * kn[1:2, :]).astype(BF16)
        vt_ref[h, 0:MLA_V, :] = v_t[h * MLA_V:(h + 1) * MLA_V, :].astype(BF16)
        vt_ref[h, MLA_V:VT_ROWS, :] = _ones_row_block(vt_ref.shape[-1])


def _ones_row_block(width):
    row = lax.broadcasted_iota(jnp.int32, (VT_ROWS - 128, width), 0)
    return jnp.where(row == 0, 1.0, 0.0).astype(BF16)


def _mla_qkv(cq, ckv, kra, krb, cos_t, sin_t, gq, gkv, wuq, wuk, wvt, qn, kn, tile):
    s = cq.shape[0]
    tm = min(256, s)
    per_tile = tile // tm
    row = lambda width: pl.BlockSpec((tm, width), lambda i: (i, 0))
    full = lambda a: pl.BlockSpec(a.shape, lambda i: (0,) * a.ndim)
    head_out = lambda width: pl.BlockSpec((MLA_HEADS, tm, width), lambda i: (0, i, 0))
    return pl.pallas_call(
        _mla_qkv_kernel,
        grid=(s // tm,),
        in_specs=[row(512), row(512), row(128), row(128), row(128), row(128),
                  full(gq), full(gkv), full(wuq), full(wuk), full(wvt), full(qn), full(kn)],
        out_specs=[head_out(HEAD_PAD), head_out(HEAD_PAD),
                   pl.BlockSpec((MLA_HEADS, None, VT_ROWS, tm), lambda i: (0, i // per_tile, 0, i % per_tile))],
        out_shape=[jax.ShapeDtypeStruct((MLA_HEADS, s, HEAD_PAD), BF16),
                   jax.ShapeDtypeStruct((MLA_HEADS, s, HEAD_PAD), BF16),
                   jax.ShapeDtypeStruct((MLA_HEADS, s // tile, VT_ROWS, tile), BF16)],
        compiler_params=_cparams("parallel"),
        name="mla_qkv",
    )(cq, ckv, kra, krb, cos_t, sin_t, gq, gkv, wuq, wuk, wvt, qn, kn)


def _flash_kernel(slope_ref, q_ref, k_ref, vt_ref, o_ref, m_ref, acc_ref, s_ref, cm_ref, *, tile, strip):
    h = pl.program_id(0)
    pair = pl.program_id(1)
    base = 2 * pair
    slope_tile = slope_ref[h] * tile
    m_ref[...] = jnp.full(m_ref.shape, -jnp.inf, F32)
    acc_ref[...] = jnp.zeros(acc_ref.shape, F32)
    n_strips = tile // strip
    strips = [(c, s) for c in range(2) for s in range(n_strips)]

    def cols(s):
        return slice(s * strip, (s + 1) * strip)

    def scores(c, s, kj, diagonal):
        q = q_ref[c * tile + s * strip:c * tile + (s + 1) * strip, :]
        st = _dot_nt(k_ref[kj], q)
        if diagonal:
            key_idx = lax.broadcasted_iota(jnp.int32, (tile, strip), 0)
            qry_idx = lax.broadcasted_iota(jnp.int32, (tile, strip), 1) + s * strip
            st = jnp.where(key_idx <= qry_idx, st, MASK_VALUE)
        return st, jnp.max(st, axis=0, keepdims=True)

    def softmax_pv(c, s, kj, st, col_max):
        shift = slope_tile * (kj - (base + c)).astype(F32)
        m_prev = m_ref[c, :, cols(s)]
        m_new = jnp.maximum(m_prev, col_max + shift)
        alpha = jnp.exp2(m_prev - m_new)
        p = jnp.exp2(st - (m_new - shift)).astype(BF16)
        acc_ref[c, :, cols(s)] = alpha * acc_ref[c, :, cols(s)] + _dot(vt_ref[kj], p)
        m_ref[c, :, cols(s)] = m_new

    def hold(c, s, st, col_max):
        s_ref[c, :, cols(s)] = st
        cm_ref[c, :, cols(s)] = col_max

    def held(c, s):
        return s_ref[c, :, cols(s)], cm_ref[c, :, cols(s)]

    pending = None
    for s in range(n_strips):
        current = scores(1, s, base, False)
        if pending is not None:
            softmax_pv(1, s - 1, base, *pending)
        pending = current
    for i, (c, s) in enumerate(strips):
        diag = scores(c, s, base + c, True)
        if i == 0:
            softmax_pv(1, n_strips - 1, base, *pending)
        hold(c, s, *diag)

    def body(j, carry):
        for c, s in strips:
            nxt = scores(c, s, j, False)
            softmax_pv(c, s, jnp.where(j == 0, base + c, j - 1), *held(c, s))
            hold(c, s, *nxt)
        return carry

    lax.fori_loop(0, base, body, 0)
    for c, s in strips:
        softmax_pv(c, s, jnp.where(base == 0, base + c, base - 1), *held(c, s))
    for c in range(2):
        acc = acc_ref[c]
        out_t = acc[0:128, :] / acc[128:129, :]
        o_ref[c * tile:(c + 1) * tile, :] = out_t.T.astype(o_ref.dtype)


def _flash(q, k, vt, slopes, *, tile, name):
    heads, s, _ = q.shape
    nk = s // tile
    return pl.pallas_call(
        functools.partial(_flash_kernel, tile=tile, strip=min(FLASH_STRIP, tile)),
        grid_spec=pltpu.PrefetchScalarGridSpec(
            num_scalar_prefetch=1,
            grid=(heads, s // (2 * tile)),
            in_specs=[pl.BlockSpec((None, 2 * tile, HEAD_PAD), lambda h, i, sl: (h, i, 0)),
                      pl.BlockSpec((None, nk, tile, HEAD_PAD), lambda h, i, sl: (h, 0, 0, 0)),
                      pl.BlockSpec((None, nk, VT_ROWS, tile), lambda h, i, sl: (h, 0, 0, 0))],
            out_specs=pl.BlockSpec((2 * tile, 128), lambda h, i, sl: (i, h)),
            scratch_shapes=[pltpu.VMEM((2, 1, tile), F32), pltpu.VMEM((2, VT_ROWS, tile), F32),
                            pltpu.VMEM((2, tile, tile), F32), pltpu.VMEM((2, 1, tile), F32)],
        ),
        out_shape=jax.ShapeDtypeStruct((s, heads * 128), BF16),
        compiler_params=_cparams("parallel", "arbitrary"),
        name=name,
    )(slopes, q, k, vt)


def _s5_kernel(u_ref, lag_ref, wx_ref, vo_ref, a_ref, d_ref, glu_ref, o_ref, carry_ref, x_ref, s_ref, m_ref):
    cb = pl.program_id(1)
    n_chunks = u_ref.shape[0] // S5_CHUNK
    half = S5_OCTET * S5_STATE

    @pl.when(cb == 0)
    def _():
        carry_ref[...] = jnp.zeros(carry_ref.shape, F32)
        for j in range(S5_CHUNK):
            for t in range(j, S5_CHUNK):
                m_ref[j * LANES:(j + 1) * LANES, t * LANES:(t + 1) * LANES] = lag_ref[t - j]
        for j in range(S5_CHUNK):
            for t in range(S5_OUT_STEPS * (j // S5_OUT_STEPS), j):
                m_ref[j * LANES:(j + 1) * LANES, t * LANES:(t + 1) * LANES] = jnp.zeros((LANES, LANES), BF16)

    steps = [u_ref[pl.ds(j, n_chunks, stride=S5_CHUNK), :] for j in range(S5_CHUNK)]
    xb = jnp.concatenate([p.astype(BF16) for p in steps], axis=1)
    x_ref[...] = _dot(xb, wx_ref[...])
    a_re = a_ref[0:1, :]
    a_im = a_ref[1:2, :]

    def body(c, carry):
        re, im = carry
        s_ref[pl.ds(c, 1), :] = jnp.concatenate([re, im], axis=1)
        x = x_ref[pl.ds(c, 1), :]
        return (a_re * re - a_im * im + x[:, :half], a_re * im + a_im * re + x[:, half:])

    re, im = lax.fori_loop(0, n_chunks, body, (carry_ref[0:1, :], carry_ref[1:2, :]), unroll=8)
    carry_ref[0:1, :] = re
    carry_ref[1:2, :] = im

    sb = s_ref[...].astype(BF16)
    d = d_ref[...]
    group = S5_OUT_STEPS
    for tg in range(S5_CHUNK // group):
        lo, hi = group * tg * LANES, group * (tg + 1) * LANES
        yg = _dot(xb[:, :hi], m_ref[0:hi, lo:hi]) + _dot(sb, vo_ref[:, lo:hi])
        ys = [jax.nn.gelu(yg[:, k * LANES:(k + 1) * LANES] + d * steps[group * tg + k]).astype(BF16)
              for k in range(group)]
        vg = _dot(jnp.concatenate(ys, axis=0), glu_ref[...])
        out = vg[:, :LANES] * jax.nn.sigmoid(vg[:, LANES:])
        for k in range(group):
            o_ref[pl.ds(group * tg + k, n_chunks, stride=S5_CHUNK), :] = out[k * n_chunks:(k + 1) * n_chunks, :]


def _s5(u, lags, wx, vo, a, d, glu):
    s = u.shape[0]
    n_chunks = s // S5_CHUNK
    cb = min(S5_CHUNK_BLOCK, n_chunks)
    octets = S5_GROUPS // S5_OCTET
    per_octet = lambda arr: pl.BlockSpec((None,) + arr.shape[1:], lambda o, i: (o,) + (0,) * (arr.ndim - 1),
                                         pipeline_mode=pl.Buffered(1))
    io_spec = pl.BlockSpec((cb * S5_CHUNK, LANES), lambda o, i: (i, o))
    state_w = 2 * S5_OCTET * S5_STATE
    return pl.pallas_call(
        _s5_kernel,
        grid=(octets, n_chunks // cb),
        in_specs=[io_spec] + [per_octet(arr) for arr in (lags, wx, vo, a, d, glu)],
        out_specs=io_spec,
        out_shape=jax.ShapeDtypeStruct((s, S5_WIDTH), F32),
        scratch_shapes=[pltpu.VMEM((2, state_w // 2), F32), pltpu.VMEM((cb, state_w), F32),
                        pltpu.VMEM((cb, state_w), F32),
                        pltpu.VMEM((S5_CHUNK * LANES, S5_CHUNK * LANES), BF16)],
        compiler_params=_cparams("parallel", "arbitrary"),
        name="s5_scan",
    )(u, lags, wx, vo, a, d, glu)


def _s5_operators(a_re, a_im, log_dt, b_re, b_im, c_re, c_im, d_skip, glu_val, glu_gate):
    hp = lax.Precision.HIGHEST
    a_re, a_im = a_re.astype(F32), a_im.astype(F32)
    dt = jnp.exp(log_dt.astype(F32))[:, None]
    er, ei = a_re * dt, a_im * dt

    def lam_pow(k):
        kk = k.astype(F32)[:, None, None]
        mag = jnp.exp(er[None] * kk)
        return mag * jnp.cos(ei[None] * kk), mag * jnp.sin(ei[None] * kk)

    l1r, l1i = lam_pow(jnp.arange(1, 2))
    nr, ni = l1r[0] - 1.0, l1i[0]
    den = a_re * a_re + a_im * a_im
    fr, fi = (nr * a_re + ni * a_im) / den, (ni * a_re - nr * a_im) / den
    b_re, b_im = b_re.astype(F32), b_im.astype(F32)
    bbr = fr[:, :, None] * b_re - fi[:, :, None] * b_im
    bbi = fr[:, :, None] * b_im + fi[:, :, None] * b_re
    c_re, c_im = c_re.astype(F32), c_im.astype(F32)

    L, A, N, P = S5_CHUNK, S5_OCTET, S5_STATE, S5_GROUP
    O = S5_GROUPS // A
    same = jnp.eye(A, dtype=bool)
    pr, pi = lam_pow(jnp.arange(0, L + 1))
    lbr = pr[:L, :, :, None] * bbr[None] - pi[:L, :, :, None] * bbi[None]
    lbi = pr[:L, :, :, None] * bbi[None] + pi[:L, :, :, None] * bbr[None]
    kern = (jnp.einsum('gpn,lgnq->lgqp', c_re, lbr, precision=hp)
            - jnp.einsum('gpn,lgnq->lgqp', c_im, lbi, precision=hp))
    lags = kern.reshape(L, O, A, P, P).transpose(1, 0, 2, 3, 4)
    lags = jnp.where(same[None, None, :, None, :, None], lags[:, :, :, :, None, :], 0.0)
    lags = lags.astype(BF16).reshape(O, L, A * P, A * P)
    rev = L - 1 - jnp.arange(L)

    def state_cols(w):
        w = w.reshape(L, O, A, N, P).transpose(1, 0, 2, 4, 3)
        w = jnp.where(same[None, None, :, None, :, None], w[:, :, :, :, None, :], 0.0)
        return w.astype(BF16).reshape(O, L * A * P, A * N)

    wx = jnp.concatenate([state_cols(lbr[rev]), state_cols(lbi[rev])], axis=-1)
    p1r, p1i = pr[1:], pi[1:]
    cr = c_re.transpose(0, 2, 1)[None]
    ci = c_im.transpose(0, 2, 1)[None]
    vr = cr * p1r[..., None] - ci * p1i[..., None]
    vi = cr * p1i[..., None] + ci * p1r[..., None]

    def state_rows(v):
        v = v.reshape(L, O, A, N, P).transpose(1, 2, 3, 0, 4)
        v = jnp.where(same[None, :, None, None, :, None], v[:, :, :, :, None, :], 0.0)
        return v.astype(BF16).reshape(O, A * N, L * A * P)

    vo = jnp.concatenate([state_rows(vr), state_rows(-vi)], axis=1)
    a = jnp.stack([pr[L].reshape(O, A * N), pi[L].reshape(O, A * N)], axis=1)
    d = d_skip.astype(F32).reshape(O, 1, A * P)

    def glu_cols(w):
        w = w.astype(F32).reshape(O, A, P, P)
        w = jnp.where(same[None, :, None, :, None], w[:, :, :, None, :], 0.0)
        return w.astype(BF16).reshape(O, A * P, A * P)

    glu = jnp.concatenate([glu_cols(glu_val), glu_cols(glu_gate)], axis=-1)
    return lags, wx, vo, a, d, glu


def _proj_residual_kernel(*refs, n_in):
    x_ref = refs[0]
    o_ref = refs[1 + 2 * n_in]
    acc = x_ref[...]
    for i in range(n_in):
        acc = acc + _dot(refs[1 + i][...].astype(BF16), refs[1 + n_in + i][...])
    o_ref[...] = acc


def _proj_residual(x, acts, weights):
    s = x.shape[0]
    tm = min(512, s)
    n_in = len(acts)
    return pl.pallas_call(
        functools.partial(_proj_residual_kernel, n_in=n_in),
        grid=(s // tm,),
        in_specs=([pl.BlockSpec((tm, D_MODEL), lambda i: (i, 0))]
                  + [pl.BlockSpec((tm, a.shape[1]), lambda i: (i, 0)) for a in acts]
                  + [pl.BlockSpec(w.shape, lambda i: (0, 0)) for w in weights]),
        out_specs=pl.BlockSpec((tm, D_MODEL), lambda i: (i, 0)),
        out_shape=jax.ShapeDtypeStruct((s, D_MODEL), F32),
        compiler_params=_cparams("parallel"),
        name="proj_residual",
    )(x, *acts, *weights)


def _mem_kv_kernel(mem_ref, g_ref, w_ref, kn_ref, k_ref, v_ref):
    kv = _dot(_rms(mem_ref[...], g_ref[...]).astype(BF16), w_ref[...])
    width = XA_HEADS * XA_HD
    for h in range(XA_HEADS):
        k_ref[h] = _rms(kv[:, h * XA_HD:(h + 1) * XA_HD], kn_ref[...]).astype(BF16)
        v_ref[h] = kv[:, width + h * XA_HD:width + (h + 1) * XA_HD].astype(BF16)


def _mem_kv(mem, g, w, kn):
    m_len = mem.shape[0]
    shape = jax.ShapeDtypeStruct((XA_HEADS, m_len, XA_HD), BF16)
    return pl.pallas_call(
        _mem_kv_kernel,
        out_shape=[shape, shape],
        compiler_params=pltpu.CompilerParams(vmem_limit_bytes=VMEM_LIMIT_BYTES),
        name="xa_mem_kv",
    )(mem, g, w, kn)


def _xa_kernel(x_ref, g_ref, wq_ref, qn_ref, k_ref, v_ref, wo_ref, o_ref):
    x = x_ref[...]
    q = _dot(_rms(x, g_ref[...]).astype(BF16), wq_ref[...])
    qn = qn_ref[...] * (XA_HD ** -0.5)
    heads = []
    for h in range(XA_HEADS):
        qh = _rms(q[:, h * XA_HD:(h + 1) * XA_HD], qn).astype(BF16)
        sc = _dot_nt(qh, k_ref[h])
        sc = sc - jnp.max(sc, axis=-1, keepdims=True)
        p = jnp.exp(sc)
        p = p / jnp.sum(p, axis=-1, keepdims=True)
        heads.append(_dot(p.astype(BF16), v_ref[h]).astype(BF16))
    o = jnp.concatenate(heads, axis=-1)
    o_ref[...] = x + _dot(o, wo_ref[...])


def _xa(x, g, wq, qn, k, v, wo):
    s = x.shape[0]
    tm = min(512, s)
    full = lambda a: pl.BlockSpec(a.shape, lambda i: (0,) * a.ndim)
    return pl.pallas_call(
        _xa_kernel,
        grid=(s // tm,),
        in_specs=[pl.BlockSpec((tm, D_MODEL), lambda i: (i, 0)), full(g), full(wq), full(qn),
                  full(k), full(v), full(wo)],
        out_specs=pl.BlockSpec((tm, D_MODEL), lambda i: (i, 0)),
        out_shape=jax.ShapeDtypeStruct((s, D_MODEL), F32),
        compiler_params=_cparams("parallel"),
        name="xa",
    )(x, g, wq, qn, k, v, wo)


def _mlp_kernel(x_ref, g_ref, wu_ref, wd_ref, o_ref, xn_ref, *, sub):
    j = pl.program_id(1)

    @pl.when(j == 0)
    def _():
        x = x_ref[...]
        xn_ref[...] = _rms(x, g_ref[...]).astype(BF16)
        o_ref[...] = x

    xn = xn_ref[...]
    n_sub = wu_ref.shape[1] // sub
    hids = [_dot(xn, wu_ref[:, i * sub:(i + 1) * sub]) for i in range(n_sub)]
    acc = o_ref[...]
    for i in range(n_sub):
        hid = jnp.maximum(hids[i], 0.0)
        acc = acc + _dot((hid * hid).astype(BF16), wd_ref[i * sub:(i + 1) * sub, :])
    o_ref[...] = acc


def _mlp(x, g, wu, wd):
    s = x.shape[0]
    tm = min(512, s)
    tf = 2048
    return pl.pallas_call(
        functools.partial(_mlp_kernel, sub=512),
        grid=(s // tm, D_FF // tf),
        in_specs=[pl.BlockSpec((tm, D_MODEL), lambda i, j: (i, 0)),
                  pl.BlockSpec((1, D_MODEL), lambda i, j: (0, 0)),
                  pl.BlockSpec((D_MODEL, tf), lambda i, j: (0, j)),
                  pl.BlockSpec((tf, D_MODEL), lambda i, j: (j, 0))],
        out_specs=pl.BlockSpec((tm, D_MODEL), lambda i, j: (i, 0)),
        out_shape=jax.ShapeDtypeStruct((s, D_MODEL), F32),
        scratch_shapes=[pltpu.VMEM((tm, D_MODEL), BF16)],
        compiler_params=_cparams("parallel", "arbitrary"),
        name="mlp",
    )(x, g, wu, wd)


def _moba_qkv_kernel(x_ref, g_ref, w_ref, wvt_ref, hn_ref, qk_ref, vt_ref, xn_ref, *,
                     heads_per_block, norm_blocks, row_split):
    j = pl.program_id(1)

    @pl.when(j == 0)
    def _():
        xn_ref[...] = _rms(x_ref[...], g_ref[...]).astype(BF16)

    @pl.when(j < norm_blocks)
    def _():
        rows = xn_ref.shape[0] // row_split
        products = [_dot(xn_ref[r * rows:(r + 1) * rows, :], w_ref[...]) for r in range(row_split)]
        for r, p in enumerate(products):
            for h in range(heads_per_block):
                sl = slice(h * MOBA_HD, (h + 1) * MOBA_HD)
                qk_ref[r * rows:(r + 1) * rows, sl] = _rms(p[:, sl], hn_ref[:, sl]).astype(qk_ref.dtype)

    @pl.when(j >= norm_blocks)
    def _():
        v_t = _dot_nt(wvt_ref[...], xn_ref[...])
        for h in range(heads_per_block):
            vt_ref[h, 0:MOBA_HD, :] = v_t[h * MOBA_HD:(h + 1) * MOBA_HD, :].astype(vt_ref.dtype)
            vt_ref[h, MOBA_HD:VT_ROWS, :] = _ones_row_block(vt_ref.shape[-1])


def _moba_qkv(x, g, w_qk, w_vt, head_gains, tile):
    s = x.shape[0]
    tm = tile
    tn = 1024
    heads_per_block = tn // MOBA_HD
    norm_blocks = w_qk.shape[1] // tn
    v_blocks = w_vt.shape[0] // tn
    kernel = functools.partial(_moba_qkv_kernel, heads_per_block=heads_per_block, norm_blocks=norm_blocks,
                               row_split=4 if tm % 1024 == 0 else 1)
    qk_blk = lambda j: jnp.minimum(j, norm_blocks - 1)
    v_blk = lambda j: jnp.maximum(j - norm_blocks, 0)
    return pl.pallas_call(
        kernel,
        grid=(s // tm, norm_blocks + v_blocks),
        in_specs=[pl.BlockSpec((tm, D_MODEL), lambda i, j: (i, 0)),
                  pl.BlockSpec((1, D_MODEL), lambda i, j: (0, 0)),
                  pl.BlockSpec((D_MODEL, tn), lambda i, j: (0, qk_blk(j))),
                  pl.BlockSpec((tn, D_MODEL), lambda i, j: (v_blk(j), 0)),
                  pl.BlockSpec((1, tn), lambda i, j: (0, qk_blk(j)))],
        out_specs=[pl.BlockSpec((tm, tn), lambda i, j: (i, qk_blk(j))),
                   pl.BlockSpec((heads_per_block, None, VT_ROWS, tm), lambda i, j: (v_blk(j), i, 0, 0))],
        out_shape=[jax.ShapeDtypeStruct((s, w_qk.shape[1]), BF16),
                   jax.ShapeDtypeStruct((MOBA_HEADS, s // tile, VT_ROWS, tile), BF16)],
        scratch_shapes=[pltpu.VMEM((tm, D_MODEL), BF16)],
        compiler_params=_cparams("parallel", "arbitrary"),
        name="moba_qkv",
    )(x, g, w_qk, w_vt, head_gains)


def _kmean_kernel(k_ref, o_ref, *, blocks):
    for b in range(blocks):
        kb = k_ref[b * MOBA_BLOCK:(b + 1) * MOBA_BLOCK, :].astype(F32)
        o_ref[b:b + 1, :] = jnp.mean(kb, axis=0, keepdims=True)


def _kmean(qk):
    s = qk.shape[0]
    width = MOBA_HEADS * MOBA_HD
    n_blk = s // MOBA_BLOCK
    blocks = 8 if n_blk % 8 == 0 else n_blk
    rows = blocks * MOBA_BLOCK
    return pl.pallas_call(
        functools.partial(_kmean_kernel, blocks=blocks),
        grid=(s // rows, width // 512),
        in_specs=[pl.BlockSpec((rows, 512), lambda i, j: (i, width // 512 + j))],
        out_specs=pl.BlockSpec((blocks, 512), lambda i, j: (i, j)),
        out_shape=jax.ShapeDtypeStruct((n_blk, width), F32),
        compiler_params=_cparams("parallel", "parallel"),
        name="moba_kmean",
    )(qk)


def _split3(x):
    hi = x.astype(BF16)
    r1 = x - hi.astype(F32)
    mid = r1.astype(BF16)
    lo = (r1 - mid.astype(F32)).astype(BF16)
    return hi.astype(F32), mid.astype(F32), lo.astype(F32)


ROUTE_LANES = 64
SLOPE_RC_LANE = 64
SLOPE_RB_LANE = 67
QUERY_BIAS_LANE = 70


def _moba_route_kernel(q_ref, k_ref, km_ref, slope_ref, slope_cols_ref, qa_ref, ka_ref, *, blocks_per_tile):
    i = pl.program_id(0)
    shape = (MOBA_BLOCK, LANES)
    lane = lax.broadcasted_iota(jnp.int32, shape, 1)
    row_f = lax.broadcasted_iota(jnp.int32, shape, 0).astype(F32)
    blk_in_tile = (i % blocks_per_tile).astype(F32)
    offset_in_tile = blk_in_tile * MOBA_BLOCK + row_f
    k_extra = jnp.where(lane == i, 1.0, 0.0)
    k_extra = jnp.where((lane >= SLOPE_RC_LANE) & (lane < SLOPE_RC_LANE + 3), row_f, k_extra)
    k_extra = jnp.where((lane >= SLOPE_RB_LANE) & (lane < SLOPE_RB_LANE + 3), blk_in_tile, k_extra)
    k_extra = jnp.where((lane >= QUERY_BIAS_LANE) & (lane < QUERY_BIAS_LANE + 3), 1.0, k_extra)
    k_extra = k_extra.astype(BF16)
    blk = lax.broadcasted_iota(jnp.int32, (LANES, MOBA_BLOCK), 0)
    blk_f = blk.astype(F32)
    for h in range(MOBA_HEADS):
        sl = slice(h * MOBA_HD, (h + 1) * MOBA_HD)
        q = q_ref[:, sl]
        gate = lax.dot_general(km_ref[h], q.astype(F32), (((1,), (1,)), ((), ())),
                               precision=lax.Precision.HIGHEST, preferred_element_type=F32)
        gate = jnp.where(blk < i, gate, -jnp.inf)
        chosen = blk == i
        for r in range(MOBA_TOPK):
            best = jnp.max(gate, axis=0, keepdims=True)
            idx = jnp.min(jnp.where(gate == best, blk_f, float(LANES)), axis=0, keepdims=True)
            pick = blk_f == idx
            chosen = jnp.logical_or(chosen, jnp.logical_and(pick, r < i))
            gate = jnp.where(pick, -jnp.inf, gate)
        q_extra = jnp.where(chosen, 0.0, MASK_VALUE).T
        q_extra = jnp.where(lane < ROUTE_LANES, q_extra, slope_cols_ref[h])
        bias = _split3(-slope_ref[h] * offset_in_tile)
        for piece in range(3):
            q_extra = jnp.where(lane == QUERY_BIAS_LANE + piece, bias[piece], q_extra)
        qa_ref[h, :, 0:128] = q
        qa_ref[h, :, 128:256] = q_extra.astype(BF16)
        ka_ref[h, :, 0:128] = k_ref[:, sl]
        ka_ref[h, :, 128:256] = k_extra


def _moba_route(qk, kmean_t, slopes, tile):
    s = qk.shape[0]
    n_blk = s // MOBA_BLOCK
    width = MOBA_HEADS * MOBA_HD
    aug = jax.ShapeDtypeStruct((MOBA_HEADS, s, HEAD_PAD), BF16)
    pieces = _split3(slopes)
    lane = jnp.arange(LANES)
    cols = jnp.zeros((MOBA_HEADS, LANES), F32)
    for piece in range(3):
        cols = jnp.where(lane[None, :] == SLOPE_RC_LANE + piece, pieces[piece][:, None], cols)
        cols = jnp.where(lane[None, :] == SLOPE_RB_LANE + piece, MOBA_BLOCK * pieces[piece][:, None], cols)
    slope_rows = jnp.broadcast_to(slopes[:, None, None], (MOBA_HEADS, 1, LANES))
    full = lambda a: pl.BlockSpec(a.shape, lambda i: (0,) * a.ndim)
    return pl.pallas_call(
        functools.partial(_moba_route_kernel, blocks_per_tile=tile // MOBA_BLOCK),
        grid=(n_blk,),
        in_specs=[pl.BlockSpec((MOBA_BLOCK, width), lambda i: (i, 0)),
                  pl.BlockSpec((MOBA_BLOCK, width), lambda i: (i, 1)),
                  full(kmean_t), full(slope_rows), full(cols[:, None, :])],
        out_specs=[pl.BlockSpec((MOBA_HEADS, MOBA_BLOCK, HEAD_PAD), lambda i: (0, i, 0))] * 2,
        out_shape=[aug, aug],
        compiler_params=_cparams("parallel"),
        name="moba_route",
    )(qk, qk, kmean_t, slope_rows, cols[:, None, :])


def _rope_tables(s):
    inv_freq = 1.0 / (ROPE_BASE ** (jnp.arange(0, MLA_ROPE, 2, dtype=F32) / MLA_ROPE))
    ang = jnp.arange(s, dtype=F32)[:, None] * inv_freq[None, :]
    cos, sin = jnp.cos(ang), jnp.sin(ang)
    zero = jnp.zeros((s, 64), F32)
    return jnp.concatenate([cos, cos, zero], axis=1), jnp.concatenate([-sin, sin, zero], axis=1)


def _rope_cols(w):
    x1, x2 = w[..., :32], w[..., 32:]
    zero = jnp.zeros(w.shape[:-1] + (64,), w.dtype)
    return jnp.concatenate([x1, x2, zero], axis=-1), jnp.concatenate([x2, x1, zero], axis=-1)


def _row(v):
    return v.astype(F32).reshape(1, -1)


def _qk_gain_rows(g):
    g = g.astype(F32)
    return jnp.stack([g[:MLA_NOPE], jnp.concatenate([g[MLA_NOPE:], jnp.zeros((64,), F32)])])


def _cross_attention_mlp(x, mem, xa_norm, xa_mem_norm, xa_w_q, xa_w_kv, xa_q_norm, xa_k_norm, xa_w_o,
                         mlp_norm, w_up, w_down):
    k, v = _mem_kv(mem, _row(xa_mem_norm), xa_w_kv.astype(BF16), _row(xa_k_norm))
    x = _xa(x, _row(xa_norm), xa_w_q.astype(BF16), _row(xa_q_norm), k, v, xa_w_o.astype(BF16))
    return _mlp(x, _row(mlp_norm), w_up.astype(BF16), w_down.astype(BF16))


def kernel(x, mem, l0_mix_norm, l0_w_in, l0_q_lora_norm, l0_w_uq, l0_kv_lora_norm, l0_w_ukv, l0_mla_q_norm, l0_mla_k_norm, l0_s5_a_re, l0_s5_a_im, l0_s5_log_dt, l0_s5_b_re, l0_s5_b_im, l0_s5_c_re, l0_s5_c_im, l0_s5_d, l0_s5_glu_val, l0_s5_glu_gate, l0_w_out, l0_xa_norm, l0_xa_mem_norm, l0_xa_w_q, l0_xa_w_kv, l0_xa_q_norm, l0_xa_k_norm, l0_xa_w_o, l0_mlp_norm, l0_w_up, l0_w_down, l1_mix_norm, l1_w_qkv, l1_q_norm, l1_k_norm, l1_w_o, l1_xa_norm, l1_xa_mem_norm, l1_xa_w_q, l1_xa_w_kv, l1_xa_q_norm, l1_xa_k_norm, l1_xa_w_o, l1_mlp_norm, l1_w_up, l1_w_down):
    batch, s, _ = x.shape
    assert batch == 1 and s % MOBA_BLOCK == 0
    x = x.reshape(s, D_MODEL)
    mem = mem.reshape(mem.shape[1], D_MODEL)
    tile = min(FLASH_TILE, s // 2)
    no_slopes = jnp.zeros((MLA_HEADS,), F32)

    w_cq = l0_w_in[:, :Q_LORA]
    w_ckv = l0_w_in[:, Q_LORA:Q_LORA + KV_LORA]
    w_kra, w_krb = _rope_cols(l0_w_in[:, Q_LORA + KV_LORA:Q_LORA + KV_LORA + MLA_ROPE])
    w_u = l0_w_in[:, Q_LORA + KV_LORA + MLA_ROPE:]
    w_in = jnp.concatenate([w_cq, w_ckv, w_kra, w_krb, w_u], axis=1).astype(BF16)
    cq, ckv, kra, krb, u = _l0_in(x, _row(l0_mix_norm), w_in)

    wuq = l0_w_uq.reshape(Q_LORA, MLA_HEADS, MLA_QK)
    wuq_a, wuq_b = _rope_cols(wuq[:, :, MLA_NOPE:])
    wuq = jnp.concatenate([wuq[:, :, :MLA_NOPE], wuq_a, wuq_b], axis=-1).reshape(Q_LORA, MLA_HEADS * 384)
    cos_t, sin_t = _rope_tables(s)
    wukv = l0_w_ukv.reshape(KV_LORA, MLA_HEADS, MLA_NOPE + MLA_V)
    wuk = wukv[:, :, :MLA_NOPE].reshape(KV_LORA, MLA_HEADS * MLA_NOPE)
    wvt = wukv[:, :, MLA_NOPE:].reshape(KV_LORA, MLA_HEADS * MLA_V).T
    q, k, vt = _mla_qkv(cq, ckv, kra, krb, cos_t, sin_t, _row(l0_q_lora_norm), _row(l0_kv_lora_norm),
                        wuq.astype(BF16), wuk.astype(BF16), wvt.astype(BF16),
                        _qk_gain_rows(l0_mla_q_norm), _qk_gain_rows(l0_mla_k_norm), tile)
    kb = k.reshape(MLA_HEADS, s // tile, tile, HEAD_PAD)
    y_a = _flash(q, kb, vt, no_slopes, tile=tile, name="flash_mla")

    ops = _s5_operators(l0_s5_a_re, l0_s5_a_im, l0_s5_log_dt, l0_s5_b_re, l0_s5_b_im,
                        l0_s5_c_re, l0_s5_c_im, l0_s5_d, l0_s5_glu_val, l0_s5_glu_gate)
    y_b = _s5(u, *ops)

    w_out = l0_w_out.astype(BF16)
    x = _proj_residual(x, [y_a, y_b], [w_out[:MLA_HEADS * MLA_V], w_out[MLA_HEADS * MLA_V:]])
    x = _cross_attention_mlp(x, mem, l0_xa_norm, l0_xa_mem_norm, l0_xa_w_q, l0_xa_w_kv, l0_xa_q_norm,
                             l0_xa_k_norm, l0_xa_w_o, l0_mlp_norm, l0_w_up, l0_w_down)

    width = MOBA_HEADS * MOBA_HD
    head_gains = jnp.concatenate([jnp.tile(l1_q_norm.astype(F32) * (MOBA_HD ** -0.5 * LOG2E), MOBA_HEADS),
                                  jnp.tile(l1_k_norm.astype(F32), MOBA_HEADS)]).reshape(1, 2 * width)
    qk, vt1 = _moba_qkv(x, _row(l1_mix_norm), l1_w_qkv[:, :2 * width].astype(BF16),
                        l1_w_qkv[:, 2 * width:].T.astype(BF16), head_gains, tile)
    kmean = _kmean(qk)
    n_blk = s // MOBA_BLOCK
    kmean_t = kmean.reshape(n_blk, MOBA_HEADS, MOBA_HD).transpose(1, 0, 2)
    kmean_t = jnp.pad(kmean_t, ((0, 0), (0, LANES - n_blk), (0, 0)))
    slopes = LOG2E * 2.0 ** (-8.0 * jnp.arange(1, MOBA_HEADS + 1, dtype=F32) / MOBA_HEADS)
    q_aug, k_aug = _moba_route(qk, kmean_t, slopes, tile)
    kb1 = k_aug.reshape(MOBA_HEADS, s // tile, tile, HEAD_PAD)
    o1 = _flash(q_aug, kb1, vt1, slopes, tile=tile, name="flash_moba")
    x = _proj_residual(x, [o1], [l1_w_o.astype(BF16)])
    x = _cross_attention_mlp(x, mem, l1_xa_norm, l1_xa_mem_norm, l1_xa_w_q, l1_xa_w_kv, l1_xa_q_norm,
                             l1_xa_k_norm, l1_xa_w_o, l1_mlp_norm, l1_w_up, l1_w_down)
    return x.reshape(1, s, D_MODEL)
```

```python
import functools
import math

import jax
import jax.numpy as jnp
from jax import lax
from jax.experimental import pallas as pl
from jax.experimental.pallas import tpu as pltpu

F32 = jnp.float32
BF16 = jnp.bfloat16

D_MODEL = 2048
EPS = 1e-6
MASK_VALUE = -1e30

MLA_HEADS = 8
MLA_NOPE = 128
MLA_ROPE = 64
MLA_QK = MLA_NOPE + MLA_ROPE
MLA_V = 128
Q_LORA = 512
KV_LORA = 512
ROPE_BASE = 10000.0
S5_WIDTH = 1024
S5_GROUP = 16
S5_GROUPS = 64
S5_STATE = 64
S5_CHUNK = 16
S5_OCTET = 8
S5_CHUNK_BLOCK = 256
S5_OUT_STEPS = 4
MOBA_HEADS = 16
MOBA_HD = 128
MOBA_BLOCK = 256
MOBA_TOPK = 3
XA_HEADS = 4
XA_HD = 128
D_FF = 4 * D_MODEL

LANES = 128
LOG2E = math.log2(math.e)
FLASH_TILE = 1024
FLASH_STRIP = 256
HEAD_PAD = 256
VT_ROWS = 136
VMEM_LIMIT_BYTES = 56 * 1024 * 1024


def _cparams(*sem):
    return pltpu.CompilerParams(dimension_semantics=sem, vmem_limit_bytes=VMEM_LIMIT_BYTES)


def _rms(x, g):
    return x * lax.rsqrt(jnp.mean(x * x, axis=-1, keepdims=True) + EPS) * g


def _dot(a, b):
    return jnp.dot(a, b, preferred_element_type=F32)


def _dot_nt(a, b):
    return lax.dot_general(a, b, (((1,), (1,)), ((), ())), preferred_element_type=F32)


def _l0_in_kernel(x_ref, g_ref, w_ref, cq_ref, ckv_ref, kra_ref, krb_ref, u_ref):
    xn = _rms(x_ref[...], g_ref[...]).astype(BF16)
    p = _dot(xn, w_ref[...])
    cq_ref[...] = p[:, 0:512]
    ckv_ref[...] = p[:, 512:1024]
    kra_ref[...] = p[:, 1024:1152]
    krb_ref[...] = p[:, 1152:1280]
    u_ref[...] = p[:, 1280:2304]


def _l0_in(x, g, w):
    s = x.shape[0]
    tm = min(512, s)
    n = w.shape[1]
    row = lambda width: pl.BlockSpec((tm, width), lambda i: (i, 0))
    return pl.pallas_call(
        _l0_in_kernel,
        grid=(s // tm,),
        in_specs=[row(D_MODEL), pl.BlockSpec((1, D_MODEL), lambda i: (0, 0)),
                  pl.BlockSpec((D_MODEL, n), lambda i: (0, 0))],
        out_specs=[row(512), row(512), row(128), row(128), row(1024)],
        out_shape=[jax.ShapeDtypeStruct((s, 512), F32), jax.ShapeDtypeStruct((s, 512), F32),
                   jax.ShapeDtypeStruct((s, 128), F32), jax.ShapeDtypeStruct((s, 128), F32),
                   jax.ShapeDtypeStruct((s, 1024), F32)],
        compiler_params=_cparams("parallel"),
        name="l0_in_proj",
    )(x, g, w)


def _mla_qkv_kernel(cq_ref, ckv_ref, kra_ref, krb_ref, cos_ref, sin_ref, gq_ref, gkv_ref,
                    wuq_ref, wuk_ref, wvt_ref, qn_ref, kn_ref, q_ref, k_ref, vt_ref):
    cos = cos_ref[...]
    sin = sin_ref[...]
    qn = qn_ref[...]
    kn = kn_ref[...]
    qp = _dot(_rms(cq_ref[...], gq_ref[...]).astype(BF16), wuq_ref[...])
    ckv_n = _rms(ckv_ref[...], gkv_ref[...]).astype(BF16)
    k_nopes = _dot(ckv_n, wuk_ref[...])
    v_t = _dot_nt(wvt_ref[...], ckv_n)
    k_rot = kra_ref[...] * cos + krb_ref[...] * sin
    k_rot_ss = jnp.sum(k_rot * k_rot, axis=-1, keepdims=True)
    q_scale = MLA_QK ** -0.5 * LOG2E
    for h in range(MLA_HEADS):
        qb = h * 384
        nope = qp[:, qb:qb + 128]
        rot = qp[:, qb + 128:qb + 256] * cos + qp[:, qb + 256:qb + 384] * sin
        ss = jnp.sum(nope * nope, axis=-1, keepdims=True) + jnp.sum(rot * rot, axis=-1, keepdims=True)
        inv = lax.rsqrt(ss * (1.0 / MLA_QK) + EPS) * q_scale
        q_ref[h, :, 0:128] = (nope * inv * qn[0:1, :]).astype(BF16)
        q_ref[h, :, 128:256] = (rot * inv * qn[1:2, :]).astype(BF16)
        k_nope = k_nopes[:, h * 128:(h + 1) * 128]
        ssk = jnp.sum(k_nope * k_nope, axis=-1, keepdims=True) + k_rot_ss
        invk = lax.rsqrt(ssk * (1.0 / MLA_QK) + EPS)
        k_ref[h, :, 0:128] = (k_nope * invk * kn[0:1, :]).astype(BF16)
        k_ref[h, :, 128:256] = (k_rot * invk * kn[1:2, :]).astype(BF16)
        vt_ref[h, 0:MLA_V, :] = v_t[h * MLA_V:(h + 1) * MLA_V, :].astype(BF16)
        vt_ref[h, MLA_V:VT_ROWS, :] = _ones_row_block(vt_ref.shape[-1])


def _ones_row_block(width):
    row = lax.broadcasted_iota(jnp.int32, (VT_ROWS - 128, width), 0)
    return jnp.where(row == 0, 1.0, 0.0).astype(BF16)


def _mla_qkv(cq, ckv, kra, krb, cos_t, sin_t, gq, gkv, wuq, wuk, wvt, qn, kn, tile):
    s = cq.shape[0]
    tm = min(256, s)
    per_tile = tile // tm
    row = lambda width: pl.BlockSpec((tm, width), lambda i: (i, 0))
    full = lambda a: pl.BlockSpec(a.shape, lambda i: (0,) * a.ndim)
    head_out = lambda width: pl.BlockSpec((MLA_HEADS, tm, width), lambda i: (0, i, 0))
    return pl.pallas_call(
        _mla_qkv_kernel,
        grid=(s // tm,),
        in_specs=[row(512), row(512), row(128), row(128), row(128), row(128),
                  full(gq), full(gkv), full(wuq), full(wuk), full(wvt), full(qn), full(kn)],
        out_specs=[head_out(HEAD_PAD), head_out(HEAD_PAD),
                   pl.BlockSpec((MLA_HEADS, None, VT_ROWS, tm), lambda i: (0, i // per_tile, 0, i % per_tile))],
        out_shape=[jax.ShapeDtypeStruct((MLA_HEADS, s, HEAD_PAD), BF16),
                   jax.ShapeDtypeStruct((MLA_HEADS, s, HEAD_PAD), BF16),
                   jax.ShapeDtypeStruct((MLA_HEADS, s // tile, VT_ROWS, tile), BF16)],
        compiler_params=_cparams("parallel"),
        name="mla_qkv",
    )(cq, ckv, kra, krb, cos_t, sin_t, gq, gkv, wuq, wuk, wvt, qn, kn)


def _flash_kernel(slope_ref, q_ref, k_ref, vt_ref, o_ref, m_ref, acc_ref, s_ref, cm_ref, *, tile, strip):
    h = pl.program_id(0)
    pair = pl.program_id(1)
    base = 2 * pair
    slope_tile = slope_ref[h] * tile
    m_ref[...] = jnp.full(m_ref.shape, -jnp.inf, F32)
    acc_ref[...] = jnp.zeros(acc_ref.shape, F32)
    n_strips = tile // strip
    strips = [(c, s) for c in range(2) for s in range(n_strips)]

    def cols(s):
        return slice(s * strip, (s + 1) * strip)

    def scores(c, s, kj, diagonal):
        q = q_ref[c * tile + s * strip:c * tile + (s + 1) * strip, :]
        st = _dot_nt(k_ref[kj], q)
        if diagonal:
            key_idx = lax.broadcasted_iota(jnp.int32, (tile, strip), 0)
            qry_idx = lax.broadcasted_iota(jnp.int32, (tile, strip), 1) + s * strip
            st = jnp.where(key_idx <= qry_idx, st, MASK_VALUE)
        return st, jnp.max(st, axis=0, keepdims=True)

    def softmax_pv(c, s, kj, st, col_max):
        shift = slope_tile * (kj - (base + c)).astype(F32)
        m_prev = m_ref[c, :, cols(s)]
        m_new = jnp.maximum(m_prev, col_max + shift)
        alpha = jnp.exp2(m_prev - m_new)
        p = jnp.exp2(st - (m_new - shift)).astype(BF16)
        acc_ref[c, :, cols(s)] = alpha * acc_ref[c, :, cols(s)] + _dot(vt_ref[kj], p)
        m_ref[c, :, cols(s)] = m_new

    def hold(c, s, st, col_max):
        s_ref[c, :, cols(s)] = st
        cm_ref[c, :, cols(s)] = col_max

    def held(c, s):
        return s_ref[c, :, cols(s)], cm_ref[c, :, cols(s)]

    pending = None
    for s in range(n_strips):
        current = scores(1, s, base, False)
        if pending is not None:
            softmax_pv(1, s - 1, base, *pending)
        pending = current
    for i, (c, s) in enumerate(strips):
        diag = scores(c, s, base + c, True)
        if i == 0:
            softmax_pv(1, n_strips - 1, base, *pending)
        hold(c, s, *diag)

    def body(j, carry):
        for c, s in strips:
            nxt = scores(c, s, j, False)
            softmax_pv(c, s, jnp.where(j == 0, base + c, j - 1), *held(c, s))
            hold(c, s, *nxt)
        return carry

    lax.fori_loop(0, base, body, 0)
    for c, s in strips:
        softmax_pv(c, s, jnp.where(base == 0, base + c, base - 1), *held(c, s))
    for c in range(2):
        acc = acc_ref[c]
        out_t = acc[0:128, :] / acc[128:129, :]
        o_ref[c * tile:(c + 1) * tile, :] = out_t.T.astype(o_ref.dtype)


def _flash(q, k, vt, slopes, *, tile, name):
    heads, s, _ = q.shape
    nk = s // tile
    return pl.pallas_call(
        functools.partial(_flash_kernel, tile=tile, strip=min(FLASH_STRIP, tile)),
        grid_spec=pltpu.PrefetchScalarGridSpec(
            num_scalar_prefetch=1,
            grid=(heads, s // (2 * tile)),
            in_specs=[pl.BlockSpec((None, 2 * tile, HEAD_PAD), lambda h, i, sl: (h, i, 0)),
                      pl.BlockSpec((None, nk, tile, HEAD_PAD), lambda h, i, sl: (h, 0, 0, 0)),
                      pl.BlockSpec((None, nk, VT_ROWS, tile), lambda h, i, sl: (h, 0, 0, 0))],
            out_specs=pl.BlockSpec((2 * tile, 128), lambda h, i, sl: (i, h)),
            scratch_shapes=[pltpu.VMEM((2, 1, tile), F32), pltpu.VMEM((2, VT_ROWS, tile), F32),
                            pltpu.VMEM((2, tile, tile), F32), pltpu.VMEM((2, 1, tile), F32)],
        ),
        out_shape=jax.ShapeDtypeStruct((s, heads * 128), BF16),
        compiler_params=_cparams("parallel", "arbitrary"),
        name=name,
    )(slopes, q, k, vt)


def _s5_kernel(u_ref, lag_ref, wc_ref, vc_ref, ex_ref, ev_ref, a_ref, d_ref, glu_ref, o_ref,
               carry_ref, x_ref, s_ref, m_ref, wx_ref, vo_ref):
    cb = pl.program_id(1)
    n_chunks = u_ref.shape[0] // S5_CHUNK
    half = S5_OCTET * S5_STATE

    @pl.when(cb == 0)
    def _():
        carry_ref[...] = jnp.zeros(carry_ref.shape, F32)
        for j in range(S5_CHUNK):
            for t in range(j, S5_CHUNK):
                m_ref[j * LANES:(j + 1) * LANES, t * LANES:(t + 1) * LANES] = lag_ref[t - j]
        for j in range(S5_CHUNK):
            for t in range(S5_OUT_STEPS * (j // S5_OUT_STEPS), j):
                m_ref[j * LANES:(j + 1) * LANES, t * LANES:(t + 1) * LANES] = jnp.zeros((LANES, LANES), BF16)
        width = 256
        for c0 in range(0, wx_ref.shape[1], width):
            rows = lax.broadcasted_iota(jnp.int32, (wx_ref.shape[0], width), 0)
            cols = lax.broadcasted_iota(jnp.int32, (wx_ref.shape[0], width), 1) + c0
            own = (rows // S5_GROUP) % S5_OCTET == (cols // S5_STATE) % S5_OCTET
            spread = _dot(wc_ref[...], ex_ref[:, c0:c0 + width])
            wx_ref[:, c0:c0 + width] = jnp.where(own, spread, 0.0).astype(BF16)
        for c0 in range(0, vo_ref.shape[1], width):
            rows = lax.broadcasted_iota(jnp.int32, (vo_ref.shape[0], width), 0)
            cols = lax.broadcasted_iota(jnp.int32, (vo_ref.shape[0], width), 1) + c0
            own = (rows // S5_STATE) % S5_OCTET == (cols // S5_GROUP) % S5_OCTET
            spread = _dot(vc_ref[...], ev_ref[:, c0:c0 + width])
            vo_ref[:, c0:c0 + width] = jnp.where(own, spread, 0.0).astype(BF16)

    steps =[u_ref[pl.ds(j, n_chunks, stride=S5_CHUNK), :] for j in range(S5_CHUNK)]
    xb = jnp.concatenate([p.astype(BF16) for p in steps], axis=1)
    x_ref[...] = _dot(xb, wx_ref[...])
    a_re = a_ref[0:1, :]
    a_im = a_ref[1:2, :]

    def body(c, carry):
        re, im = carry
        s_ref[pl.ds(c, 1), :] = jnp.concatenate([re, im], axis=1)
        x = x_ref[pl.ds(c, 1), :]
        return (a_re * re - a_im * im + x[:, :half], a_re * im + a_im * re + x[:, half:])

    re, im = lax.fori_loop(0, n_chunks, body, (carry_ref[0:1, :], carry_ref[1:2, :]), unroll=8)
    carry_ref[0:1, :] = re
    carry_ref[1:2, :] = im

    sb = s_ref[...].astype(BF16)
    d = d_ref[...]
    group = S5_OUT_STEPS
    for tg in range(S5_CHUNK // group):
        lo, hi = group * tg * LANES, group * (tg + 1) * LANES
        yg = _dot(xb[:, :hi], m_ref[0:hi, lo:hi]) + _dot(sb, vo_ref[:, lo:hi])
        ys = [jax.nn.gelu(yg[:, k * LANES:(k + 1) * LANES] + d * steps[group * tg + k]).astype(BF16)
              for k in range(group)]
        vg = _dot(jnp.concatenate(ys, axis=0), glu_ref[...])
        out = vg[:, :LANES] * jax.nn.sigmoid(vg[:, LANES:])
        for k in range(group):
            o_ref[pl.ds(group * tg + k, n_chunks, stride=S5_CHUNK), :] = out[k * n_chunks:(k + 1) * n_chunks, :]


def _s5(u, lags, wc, vc, a, d, glu):
    s = u.shape[0]
    n_chunks = s // S5_CHUNK
    cb = min(S5_CHUNK_BLOCK, n_chunks)
    octets = S5_GROUPS // S5_OCTET
    per_octet = lambda arr: pl.BlockSpec((None,) + arr.shape[1:], lambda o, i: (o,) + (0,) * (arr.ndim - 1),
                                         pipeline_mode=pl.Buffered(1))
    io_spec = pl.BlockSpec((cb * S5_CHUNK, LANES), lambda o, i: (i, o))
    state_w = 2 * S5_OCTET * S5_STATE
    row_w = S5_CHUNK * LANES
    ex = (jnp.arange(2 * S5_STATE)[:, None] == (jnp.arange(state_w)[None, :] // (S5_OCTET * S5_STATE)) * S5_STATE
          + jnp.arange(state_w)[None, :] % S5_STATE).astype(BF16)
    ev = (jnp.arange(S5_CHUNK * S5_GROUP)[:, None] == (jnp.arange(row_w)[None, :] // LANES) * S5_GROUP
          + jnp.arange(row_w)[None, :] % S5_GROUP).astype(BF16)
    full = lambda arr: pl.BlockSpec(arr.shape, lambda o, i: (0,) * arr.ndim, pipeline_mode=pl.Buffered(1))
    return pl.pallas_call(
        _s5_kernel,
        grid=(octets, n_chunks // cb),
        in_specs=([io_spec, per_octet(lags), per_octet(wc), per_octet(vc), full(ex), full(ev)]
                  + [per_octet(arr) for arr in (a, d, glu)]),
        out_specs=io_spec,
        out_shape=jax.ShapeDtypeStruct((s, S5_WIDTH), F32),
        scratch_shapes=[pltpu.VMEM((2, state_w // 2), F32), pltpu.VMEM((cb, state_w), F32),
                        pltpu.VMEM((cb, state_w), F32),
                        pltpu.VMEM((row_w, row_w), BF16), pltpu.VMEM((row_w, state_w), BF16),
                        pltpu.VMEM((state_w, row_w), BF16)],
        compiler_params=_cparams("parallel", "arbitrary"),
        name="s5_scan",
    )(u, lags, wc, vc, ex, ev, a, d, glu)


def _s5_operators(a_re, a_im, log_dt, b_re, b_im, c_re, c_im, d_skip, glu_val, glu_gate):
    hp = lax.Precision.HIGHEST
    a_re, a_im = a_re.astype(F32), a_im.astype(F32)
    dt = jnp.exp(log_dt.astype(F32))[:, None]
    er, ei = a_re * dt, a_im * dt

    def lam_pow(k):
        kk = k.astype(F32)[:, None, None]
        mag = jnp.exp(er[None] * kk)
        return mag * jnp.cos(ei[None] * kk), mag * jnp.sin(ei[None] * kk)

    l1r, l1i = lam_pow(jnp.arange(1, 2))
    nr, ni = l1r[0] - 1.0, l1i[0]
    den = a_re * a_re + a_im * a_im
    fr, fi = (nr * a_re + ni * a_im) / den, (ni * a_re - nr * a_im) / den
    b_re, b_im = b_re.astype(F32), b_im.astype(F32)
    bbr = fr[:, :, None] * b_re - fi[:, :, None] * b_im
    bbi = fr[:, :, None] * b_im + fi[:, :, None] * b_re
    c_re, c_im = c_re.astype(F32), c_im.astype(F32)

    L, A, N, P = S5_CHUNK, S5_OCTET, S5_STATE, S5_GROUP
    O = S5_GROUPS // A
    same = jnp.eye(A, dtype=bool)
    pr, pi = lam_pow(jnp.arange(0, L + 1))
    lbr = pr[:L, :, :, None] * bbr[None] - pi[:L, :, :, None] * bbi[None]
    lbi = pr[:L, :, :, None] * bbi[None] + pi[:L, :, :, None] * bbr[None]
    kern = (jnp.einsum('gpn,lgnq->lgqp', c_re, lbr, precision=hp)
            - jnp.einsum('gpn,lgnq->lgqp', c_im, lbi, precision=hp))
    lags = kern.reshape(L, O, A, P, P).transpose(1, 0, 2, 3, 4)
    lags = jnp.where(same[None, None, :, None, :, None], lags[:, :, :, :, None, :], 0.0)
    lags = lags.astype(BF16).reshape(O, L, A * P, A * P)
    rev = L - 1 - jnp.arange(L)

    def state_cols(w):
        return w.reshape(L, O, A, N, P).transpose(1, 0, 2, 4, 3).reshape(O, L * A * P, N)

    wc = jnp.concatenate([state_cols(lbr[rev]), state_cols(lbi[rev])], axis=-1).astype(BF16)
    p1r, p1i = pr[1:], pi[1:]
    cr = c_re.transpose(0, 2, 1)[None]
    ci = c_im.transpose(0, 2, 1)[None]
    vr = cr * p1r[..., None] - ci * p1i[..., None]
    vi = cr * p1i[..., None] + ci * p1r[..., None]

    def state_rows(v):
        return v.reshape(L, O, A, N, P).transpose(1, 2, 3, 0, 4).reshape(O, A * N, L * P)

    vc = jnp.concatenate([state_rows(vr), state_rows(-vi)], axis=1).astype(BF16)
    a = jnp.stack([pr[L].reshape(O, A * N), pi[L].reshape(O, A * N)], axis=1)
    d = d_skip.astype(F32).reshape(O, 1, A * P)

    def glu_cols(w):
        w = w.astype(F32).reshape(O, A, P, P)
        w = jnp.where(same[None, :, None, :, None], w[:, :, :, None, :], 0.0)
        return w.astype(BF16).reshape(O, A * P, A * P)

    glu = jnp.concatenate([glu_cols(glu_val), glu_cols(glu_gate)], axis=-1)
    return lags, wc, vc, a, d, glu


def _proj_residual_kernel(*refs, n_in):
    x_ref = refs[0]
    o_ref = refs[1 + 2 * n_in]
    acc = x_ref[...]
    for i in range(n_in):
        acc = acc + _dot(refs[1 + i][...].astype(BF16), refs[1 + n_in + i][...])
    o_ref[...] = acc


def _proj_residual(x, acts, weights):
    s = x.shape[0]
    tm = min(512, s)
    n_in = len(acts)
    return pl.pallas_call(
        functools.partial(_proj_residual_kernel, n_in=n_in),
        grid=(s // tm,),
        in_specs=([pl.BlockSpec((tm, D_MODEL), lambda i: (i, 0))]
                  + [pl.BlockSpec((tm, a.shape[1]), lambda i: (i, 0)) for a in acts]
                  + [pl.BlockSpec(w.shape, lambda i: (0, 0)) for w in weights]),
        out_specs=pl.BlockSpec((tm, D_MODEL), lambda i: (i, 0)),
        out_shape=jax.ShapeDtypeStruct((s, D_MODEL), F32),
        compiler_params=_cparams("parallel"),
        name="proj_residual",
    )(x, *acts, *weights)


def _mem_kv_kernel(mem_ref, g_ref, w_ref, kn_ref, k_ref, v_ref):
    kv = _dot(_rms(mem_ref[...], g_ref[...]).astype(BF16), w_ref[...])
    width = XA_HEADS * XA_HD
    for h in range(XA_HEADS):
        k_ref[h] = _rms(kv[:, h * XA_HD:(h + 1) * XA_HD], kn_ref[...]).astype(BF16)
        v_ref[h] = kv[:, width + h * XA_HD:width + (h + 1) * XA_HD].astype(BF16)


def _mem_kv(mem, g, w, kn):
    m_len = mem.shape[0]
    shape = jax.ShapeDtypeStruct((XA_HEADS, m_len, XA_HD), BF16)
    return pl.pallas_call(
        _mem_kv_kernel,
        out_shape=[shape, shape],
        compiler_params=pltpu.CompilerParams(vmem_limit_bytes=VMEM_LIMIT_BYTES),
        name="xa_mem_kv",
    )(mem, g, w, kn)


def _xa_kernel(x_ref, g_ref, wq_ref, qn_ref, k_ref, v_ref, wo_ref, o_ref):
    x = x_ref[...]
    q = _dot(_rms(x, g_ref[...]).astype(BF16), wq_ref[...])
    qn = qn_ref[...] * (XA_HD ** -0.5)
    heads = []
    for h in range(XA_HEADS):
        qh = _rms(q[:, h * XA_HD:(h + 1) * XA_HD], qn).astype(BF16)
        sc = _dot_nt(qh, k_ref[h])
        sc = sc - jnp.max(sc, axis=-1, keepdims=True)
        p = jnp.exp(sc)
        p = p / jnp.sum(p, axis=-1, keepdims=True)
        heads.append(_dot(p.astype(BF16), v_ref[h]).astype(BF16))
    o = jnp.concatenate(heads, axis=-1)
    o_ref[...] = x + _dot(o, wo_ref[...])


def _xa(x, g, wq, qn, k, v, wo):
    s = x.shape[0]
    tm = min(512, s)
    full = lambda a: pl.BlockSpec(a.shape, lambda i: (0,) * a.ndim)
    return pl.pallas_call(
        _xa_kernel,
        grid=(s // tm,),
        in_specs=[pl.BlockSpec((tm, D_MODEL), lambda i: (i, 0)), full(g), full(wq), full(qn),
                  full(k), full(v), full(wo)],
        out_specs=pl.BlockSpec((tm, D_MODEL), lambda i: (i, 0)),
        out_shape=jax.ShapeDtypeStruct((s, D_MODEL), F32),
        compiler_params=_cparams("parallel"),
        name="xa",
    )(x, g, wq, qn, k, v, wo)


def _mlp_kernel(x_ref, g_ref, wu_ref, wd_ref, o_ref, xn_ref, *, sub):
    j = pl.program_id(1)

    @pl.when(j == 0)
    def _():
        x = x_ref[...]
        xn_ref[...] = _rms(x, g_ref[...]).astype(BF16)
        o_ref[...] = x

    xn = xn_ref[...]
    n_sub = wu_ref.shape[1] // sub
    hids = [_dot(xn, wu_ref[:, i * sub:(i + 1) * sub]) for i in range(n_sub)]
    acc = o_ref[...]
    for i in range(n_sub):
        hid = jnp.maximum(hids[i], 0.0)
        acc = acc + _dot((hid * hid).astype(BF16), wd_ref[i * sub:(i + 1) * sub, :])
    o_ref[...] = acc


def _mlp(x, g, wu, wd):
    s = x.shape[0]
    tm = min(512, s)
    tf = 2048
    return pl.pallas_call(
        functools.partial(_mlp_kernel, sub=512),
        grid=(s // tm, D_FF // tf),
        in_specs=[pl.BlockSpec((tm, D_MODEL), lambda i, j: (i, 0)),
                  pl.BlockSpec((1, D_MODEL), lambda i, j: (0, 0)),
                  pl.BlockSpec((D_MODEL, tf), lambda i, j: (0, j)),
                  pl.BlockSpec((tf, D_MODEL), lambda i, j: (j, 0))],
        out_specs=pl.BlockSpec((tm, D_MODEL), lambda i, j: (i, 0)),
        out_shape=jax.ShapeDtypeStruct((s, D_MODEL), F32),
        scratch_shapes=[pltpu.VMEM((tm, D_MODEL), BF16)],
        compiler_params=_cparams("parallel", "arbitrary"),
        name="mlp",
    )(x, g, wu, wd)


def _moba_qkv_kernel(x_ref, g_ref, w_ref, wvt_ref, hn_ref, qk_ref, vt_ref, xn_ref, *,
                     heads_per_block, norm_blocks, row_split):
    j = pl.program_id(1)

    @pl.when(j == 0)
    def _():
        xn_ref[...] = _rms(x_ref[...], g_ref[...]).astype(BF16)

    @pl.when(j < norm_blocks)
    def _():
        rows = xn_ref.shape[0] // row_split
        products = [_dot(xn_ref[r * rows:(r + 1) * rows, :], w_ref[...]) for r in range(row_split)]
        for r, p in enumerate(products):
            for h in range(heads_per_block):
                sl = slice(h * MOBA_HD, (h + 1) * MOBA_HD)
                qk_ref[r * rows:(r + 1) * rows, sl] = _rms(p[:, sl], hn_ref[:, sl]).astype(qk_ref.dtype)

    @pl.when(j >= norm_blocks)
    def _():
        v_t = _dot_nt(wvt_ref[...], xn_ref[...])
        for h in range(heads_per_block):
            vt_ref[h, 0:MOBA_HD, :] = v_t[h * MOBA_HD:(h + 1) * MOBA_HD, :].astype(vt_ref.dtype)
            vt_ref[h, MOBA_HD:VT_ROWS, :] = _ones_row_block(vt_ref.shape[-1])


def _moba_qkv(x, g, w_qk, w_vt, head_gains, tile):
    s = x.shape[0]
    tm = tile
    tn = 1024
    heads_per_block = tn // MOBA_HD
    norm_blocks = w_qk.shape[1] // tn
    v_blocks = w_vt.shape[0] // tn
    kernel = functools.partial(_moba_qkv_kernel, heads_per_block=heads_per_block, norm_blocks=norm_blocks,
                               row_split=4 if tm % 1024 == 0 else 1)
    qk_blk = lambda j: jnp.minimum(j, norm_blocks - 1)
    v_blk = lambda j: jnp.maximum(j - norm_blocks, 0)
    return pl.pallas_call(
        kernel,
        grid=(s // tm, norm_blocks + v_blocks),
        in_specs=[pl.BlockSpec((tm, D_MODEL), lambda i, j: (i, 0)),
                  pl.BlockSpec((1, D_MODEL), lambda i, j: (0, 0)),
                  pl.BlockSpec((D_MODEL, tn), lambda i, j: (0, qk_blk(j))),
                  pl.BlockSpec((tn, D_MODEL), lambda i, j: (v_blk(j), 0)),
                  pl.BlockSpec((1, tn), lambda i, j: (0, qk_blk(j)))],
        out_specs=[pl.BlockSpec((tm, tn), lambda i, j: (i, qk_blk(j))),
                   pl.BlockSpec((heads_per_block, None, VT_ROWS, tm), lambda i, j: (v_blk(j), i, 0, 0))],
        out_shape=[jax.ShapeDtypeStruct((s, w_qk.shape[1]), BF16),
                   jax.ShapeDtypeStruct((MOBA_HEADS, s // tile, VT_ROWS, tile), BF16)],
        scratch_shapes=[pltpu.VMEM((tm, D_MODEL), BF16)],
        compiler_params=_cparams("parallel", "arbitrary"),
        name="moba_qkv",
    )(x, g, w_qk, w_vt, head_gains)


def _kmean_kernel(k_ref, o_ref, *, blocks):
    for b in range(blocks):
        kb = k_ref[b * MOBA_BLOCK:(b + 1) * MOBA_BLOCK, :].astype(F32)
        o_ref[b:b + 1, :] = jnp.mean(kb, axis=0, keepdims=True)


def _kmean(qk):
    s = qk.shape[0]
    width = MOBA_HEADS * MOBA_HD
    n_blk = s // MOBA_BLOCK
    blocks = 8 if n_blk % 8 == 0 else n_blk
    rows = blocks * MOBA_BLOCK
    return pl.pallas_call(
        functools.partial(_kmean_kernel, blocks=blocks),
        grid=(s // rows, width // 512),
        in_specs=[pl.BlockSpec((rows, 512), lambda i, j: (i, width // 512 + j))],
        out_specs=pl.BlockSpec((blocks, 512), lambda i, j: (i, j)),
        out_shape=jax.ShapeDtypeStruct((n_blk, width), F32),
        compiler_params=_cparams("parallel", "parallel"),
        name="moba_kmean",
    )(qk)


def _split3(x):
    hi = x.astype(BF16)
    r1 = x - hi.astype(F32)
    mid = r1.astype(BF16)
    lo = (r1 - mid.astype(F32)).astype(BF16)
    return hi.astype(F32), mid.astype(F32), lo.astype(F32)


ROUTE_LANES = 64
SLOPE_RC_LANE = 64
SLOPE_RB_LANE = 67
QUERY_BIAS_LANE = 70


def _moba_route_kernel(q_ref, k_ref, km_ref, slope_ref, slope_cols_ref, qa_ref, ka_ref, *, blocks_per_tile):
    i = pl.program_id(0)
    shape = (MOBA_BLOCK, LANES)
    lane = lax.broadcasted_iota(jnp.int32, shape, 1)
    row_f = lax.broadcasted_iota(jnp.int32, shape, 0).astype(F32)
    blk_in_tile = (i % blocks_per_tile).astype(F32)
    offset_in_tile = blk_in_tile * MOBA_BLOCK + row_f
    k_extra = jnp.where(lane == i, 1.0, 0.0)
    k_extra = jnp.where((lane >= SLOPE_RC_LANE) & (lane < SLOPE_RC_LANE + 3), row_f, k_extra)
    k_extra = jnp.where((lane >= SLOPE_RB_LANE) & (lane < SLOPE_RB_LANE + 3), blk_in_tile, k_extra)
    k_extra = jnp.where((lane >= QUERY_BIAS_LANE) & (lane < QUERY_BIAS_LANE + 3), 1.0, k_extra)
    k_extra = k_extra.astype(BF16)
    blk = lax.broadcasted_iota(jnp.int32, (LANES, MOBA_BLOCK), 0)
    blk_f = blk.astype(F32)
    for h in range(MOBA_HEADS):
        sl = slice(h * MOBA_HD, (h + 1) * MOBA_HD)
        q = q_ref[:, sl]
        gate = lax.dot_general(km_ref[h], q.astype(F32), (((1,), (1,)), ((), ())),
                               precision=lax.Precision.HIGHEST, preferred_element_type=F32)
        gate = jnp.where(blk < i, gate, -jnp.inf)
        chosen = blk == i
        for r in range(MOBA_TOPK):
            best = jnp.max(gate, axis=0, keepdims=True)
            idx = jnp.min(jnp.where(gate == best, blk_f, float(LANES)), axis=0, keepdims=True)
            pick = blk_f == idx
            chosen = jnp.logical_or(chosen, jnp.logical_and(pick, r < i))
            gate = jnp.where(pick, -jnp.inf, gate)
        q_extra = jnp.where(chosen, 0.0, MASK_VALUE).T
        q_extra = jnp.where(lane < ROUTE_LANES, q_extra, slope_cols_ref[h])
        bias = _split3(-slope_ref[h] * offset_in_tile)
        for piece in range(3):
            q_extra = jnp.where(lane == QUERY_BIAS_LANE + piece, bias[piece], q_extra)
        qa_ref[h, :, 0:128] = q
        qa_ref[h, :, 128:256] = q_extra.astype(BF16)
        ka_ref[h, :, 0:128] = k_ref[:, sl]
        ka_ref[h, :, 128:256] = k_extra


def _moba_route(qk, kmean_t, slopes, tile):
    s = qk.shape[0]
    n_blk = s // MOBA_BLOCK
    width = MOBA_HEADS * MOBA_HD
    aug = jax.ShapeDtypeStruct((MOBA_HEADS, s, HEAD_PAD), BF16)
    pieces = _split3(slopes)
    lane = jnp.arange(LANES)
    cols = jnp.zeros((MOBA_HEADS, LANES), F32)
    for piece in range(3):
        cols = jnp.where(lane[None, :] == SLOPE_RC_LANE + piece, pieces[piece][:, None], cols)
        cols = jnp.where(lane[None, :] == SLOPE_RB_LANE + piece, MOBA_BLOCK * pieces[piece][:, None], cols)
    slope_rows = jnp.broadcast_to(slopes[:, None, None], (MOBA_HEADS, 1, LANES))
    full = lambda a: pl.BlockSpec(a.shape, lambda i: (0,) * a.ndim)
    return pl.pallas_call(
        functools.partial(_moba_route_kernel, blocks_per_tile=tile // MOBA_BLOCK),
        grid=(n_blk,),
        in_specs=[pl.BlockSpec((MOBA_BLOCK, width), lambda i: (i, 0)),
                  pl.BlockSpec((MOBA_BLOCK, width), lambda i: (i, 1)),
                  full(kmean_t), full(slope_rows), full(cols[:, None, :])],
        out_specs=[pl.BlockSpec((MOBA_HEADS, MOBA_BLOCK, HEAD_PAD), lambda i: (0, i, 0))] * 2,
        out_shape=[aug, aug],
        compiler_params=_cparams("parallel"),
        name="moba_route",
    )(qk, qk, kmean_t, slope_rows, cols[:, None, :])


def _rope_tables(s):
    inv_freq = 1.0 / (ROPE_BASE ** (jnp.arange(0, MLA_ROPE, 2, dtype=F32) / MLA_ROPE))
    ang = jnp.arange(s, dtype=F32)[:, None] * inv_freq[None, :]
    cos, sin = jnp.cos(ang), jnp.sin(ang)
    zero = jnp.zeros((s, 64), F32)
    return jnp.concatenate([cos, cos, zero], axis=1), jnp.concatenate([-sin, sin, zero], axis=1)


def _rope_cols(w):
    x1, x2 = w[..., :32], w[..., 32:]
    zero = jnp.zeros(w.shape[:-1] + (64,), w.dtype)
    return jnp.concatenate([x1, x2, zero], axis=-1), jnp.concatenate([x2, x1, zero], axis=-1)


def _row(v):
    return v.astype(F32).reshape(1, -1)


def _qk_gain_rows(g):
    g = g.astype(F32)
    return jnp.stack([g[:MLA_NOPE], jnp.concatenate([g[MLA_NOPE:], jnp.zeros((64,), F32)])])


def _cross_attention_mlp(x, mem, xa_norm, xa_mem_norm, xa_w_q, xa_w_kv, xa_q_norm, xa_k_norm, xa_w_o,
                         mlp_norm, w_up, w_down):
    k, v = _mem_kv(mem, _row(xa_mem_norm), xa_w_kv.astype(BF16), _row(xa_k_norm))
    x = _xa(x, _row(xa_norm), xa_w_q.astype(BF16), _row(xa_q_norm), k, v, xa_w_o.astype(BF16))
    return _mlp(x, _row(mlp_norm), w_up.astype(BF16), w_down.astype(BF16))


def kernel(x, mem, l0_mix_norm, l0_w_in, l0_q_lora_norm, l0_w_uq, l0_kv_lora_norm, l0_w_ukv, l0_mla_q_norm, l0_mla_k_norm, l0_s5_a_re, l0_s5_a_im, l0_s5_log_dt, l0_s5_b_re, l0_s5_b_im, l0_s5_c_re, l0_s5_c_im, l0_s5_d, l0_s5_glu_val, l0_s5_glu_gate, l0_w_out, l0_xa_norm, l0_xa_mem_norm, l0_xa_w_q, l0_xa_w_kv, l0_xa_q_norm, l0_xa_k_norm, l0_xa_w_o, l0_mlp_norm, l0_w_up, l0_w_down, l1_mix_norm, l1_w_qkv, l1_q_norm, l1_k_norm, l1_w_o, l1_xa_norm, l1_xa_mem_norm, l1_xa_w_q, l1_xa_w_kv, l1_xa_q_norm, l1_xa_k_norm, l1_xa_w_o, l1_mlp_norm, l1_w_up, l1_w_down):
    batch, s, _ = x.shape
    assert batch == 1 and s % MOBA_BLOCK == 0
    x = x.reshape(s, D_MODEL)
    mem = mem.reshape(mem.shape[1], D_MODEL)
    tile = min(FLASH_TILE, s // 2)
    no_slopes = jnp.zeros((MLA_HEADS,), F32)

    w_cq = l0_w_in[:, :Q_LORA]
    w_ckv = l0_w_in[:, Q_LORA:Q_LORA + KV_LORA]
    w_kra, w_krb = _rope_cols(l0_w_in[:, Q_LORA + KV_LORA:Q_LORA + KV_LORA + MLA_ROPE])
    w_u = l0_w_in[:, Q_LORA + KV_LORA + MLA_ROPE:]
    w_in = jnp.concatenate([w_cq, w_ckv, w_kra, w_krb, w_u], axis=1).astype(BF16)
    cq, ckv, kra, krb, u = _l0_in(x, _row(l0_mix_norm), w_in)

    wuq = l0_w_uq.reshape(Q_LORA, MLA_HEADS, MLA_QK)
    wuq_a, wuq_b = _rope_cols(wuq[:, :, MLA_NOPE:])
    wuq = jnp.concatenate([wuq[:, :, :MLA_NOPE], wuq_a, wuq_b], axis=-1).reshape(Q_LORA, MLA_HEADS * 384)
    cos_t, sin_t = _rope_tables(s)
    wukv = l0_w_ukv.reshape(KV_LORA, MLA_HEADS, MLA_NOPE + MLA_V)
    wuk = wukv[:, :, :MLA_NOPE].reshape(KV_LORA, MLA_HEADS * MLA_NOPE)
    wvt = wukv[:, :, MLA_NOPE:].reshape(KV_LORA, MLA_HEADS * MLA_V).T
    q, k, vt = _mla_qkv(cq, ckv, kra, krb, cos_t, sin_t, _row(l0_q_lora_norm), _row(l0_kv_lora_norm),
                        wuq.astype(BF16), wuk.astype(BF16), wvt.astype(BF16),
                        _qk_gain_rows(l0_mla_q_norm), _qk_gain_rows(l0_mla_k_norm), tile)
    kb = k.reshape(MLA_HEADS, s // tile, tile, HEAD_PAD)
    y_a = _flash(q, kb, vt, no_slopes, tile=tile, name="flash_mla")

    ops = _s5_operators(l0_s5_a_re, l0_s5_a_im, l0_s5_log_dt, l0_s5_b_re, l0_s5_b_im,
                        l0_s5_c_re, l0_s5_c_im, l0_s5_d, l0_s5_glu_val, l0_s5_glu_gate)
    y_b = _s5(u, *ops)

    w_out = l0_w_out.astype(BF16)
    x = _proj_residual(x, [y_a, y_b], [w_out[:MLA_HEADS * MLA_V], w_out[MLA_HEADS * MLA_V:]])
    x = _cross_attention_mlp(x, mem, l0_xa_norm, l0_xa_mem_norm, l0_xa_w_q, l0_xa_w_kv, l0_xa_q_norm,
                             l0_xa_k_norm, l0_xa_w_o, l0_mlp_norm, l0_w_up, l0_w_down)

    width = MOBA_HEADS * MOBA_HD
    head_gains = jnp.concatenate([jnp.tile(l1_q_norm.astype(F32) * (MOBA_HD ** -0.5 * LOG2E), MOBA_HEADS),
                                  jnp.tile(l1_k_norm.astype(F32), MOBA_HEADS)]).reshape(1, 2 * width)
    qk, vt1 = _moba_qkv(x, _row(l1_mix_norm), l1_w_qkv[:, :2 * width].astype(BF16),
                        l1_w_qkv[:, 2 * width:].T.astype(BF16), head_gains, tile)
    kmean = _kmean(qk)
    n_blk = s // MOBA_BLOCK
    kmean_t = kmean.reshape(n_blk, MOBA_HEADS, MOBA_HD).transpose(1, 0, 2)
    kmean_t = jnp.pad(kmean_t, ((0, 0), (0, LANES - n_blk), (0, 0)))
    slopes = LOG2E * 2.0 ** (-8.0 * jnp.arange(1, MOBA_HEADS + 1, dtype=F32) / MOBA_HEADS)
    q_aug, k_aug = _moba_route(qk, kmean_t, slopes, tile)
    kb1 = k_aug.reshape(MOBA_HEADS, s // tile, tile, HEAD_PAD)
    o1 = _flash(q_aug, kb1, vt1, slopes, tile=tile, name="flash_moba")
    x = _proj_residual(x, [o1], [l1_w_o.astype(BF16)])
    x = _cross_attention_mlp(x, mem, l1_xa_norm, l1_xa_mem_norm, l1_xa_w_q, l1_xa_w_kv, l1_xa_q_norm,
                             l1_xa_k_norm, l1_xa_w_o, l1_mlp_norm, l1_w_up, l1_w_down)
    return x.reshape(1, s, D_MODEL)
```
